```python
import jax
import jax.numpy as jnp
from jax import lax
import numpy as np

D_MODEL = 1024
BATCH = 8
SEQ = 2048
DEPTH = 1
DEC_BATCH = 16
DEC_SEQ = 16
PAST_LEN = 4096

CHUNK = 64
Q_BLOCK = 128
TOK_BLOCK = 128
EPS = 1e-6

H_A = 8
KVH_A = 2
GRP_A = H_A // KVH_A
DH_A = 64
H_IDX = 8
D_IDX = 64
TOPK_MAX = 256

H_B = 4
DK_B = 64
DV_B = 128
ROT_BASE = 10000.0

W_A = H_A * DH_A
W_B = H_B * DV_B
D_MIX = W_A + W_B

SPLIT_SIZES = (H_A * DH_A, KVH_A * DH_A, KVH_A * DH_A, H_IDX * D_IDX, D_IDX, H_IDX,
               H_B * DK_B, H_B * DK_B, H_B * DV_B, W_B)
D_IN = H_A * DH_A + 2 * KVH_A * DH_A + H_IDX * D_IDX + D_IDX + H_IDX + 2 * H_B * DK_B + H_B * DV_B + W_B

N_KEYS = 128
N_EXPERTS = N_KEYS * N_KEYS
PEER_HEADS = 8
PEER_TOPK = 16
D_PKEY = 256
D_PHALF = D_PKEY // 2

kernel_name = 'hybrid_dsa_retention_peer_stream_step'


def rmsnorm(x, g):
    xf = x.astype(jnp.float32)
    xf = xf * lax.rsqrt(jnp.mean(xf * xf, axis=-1, keepdims=True) + EPS)
    return xf.astype(x.dtype) * g


def project(h, w_in):
    B, T, _ = h.shape
    offs = []
    acc = 0
    for s in SPLIT_SIZES[:-1]:
        acc += s
        offs.append(acc)
    q_a, k_a, v_a, q_i, k_i, w_i, q_b, k_b, v_b, g_b = jnp.split(h @ w_in, offs, axis=-1)
    return (q_a.reshape(B, T, H_A, DH_A), k_a.reshape(B, T, KVH_A, DH_A), v_a.reshape(B, T, KVH_A, DH_A),
            q_i.reshape(B, T, H_IDX, D_IDX), k_i, w_i,
            q_b.reshape(B, T, H_B, DK_B), k_b.reshape(B, T, H_B, DK_B), v_b.reshape(B, T, H_B, DV_B), g_b)


def dsa_block(q, qi, wi, qpos, k_all, v_all, ki_all, kpos, topk):
    B, Tq = q.shape[:2]
    dots = jnp.einsum('bthd,bsd->bths', qi.astype(jnp.float32), ki_all.astype(jnp.float32))
    score = jnp.einsum('bth,bths->bts', wi.astype(jnp.float32), jax.nn.relu(dots))
    allowed = (kpos[None, :] // CHUNK) <= (qpos[:, None] // CHUNK)
    score = jnp.where(allowed[None], score, -jnp.inf)
    top_s, idx = lax.top_k(score, topk)
    valid = jnp.isfinite(top_s)
    gather = jax.vmap(lambda a, i: a[i])
    k_sel = gather(k_all, idx)
    v_sel = gather(v_all, idx)
    qg = q.reshape(B, Tq, KVH_A, GRP_A, DH_A)
    s = jnp.einsum('btkgd,btnkd->btkgn', qg, k_sel).astype(jnp.float32) * (DH_A ** -0.5)
    s = jnp.where(valid[:, :, None, None, :], s, -jnp.inf)
    p = jax.nn.softmax(s, axis=-1)
    o = jnp.einsum('btkgn,btnkd->btkgd', p.astype(v_sel.dtype), v_sel)
    return o.reshape(B, Tq, W_A)


def dsa_prompt(q, qi, wi, k, v, ki):
    B, T = q.shape[:2]
    nb = T // Q_BLOCK
    topk = min(TOPK_MAX, T // 4)
    pos = jnp.arange(T, dtype=jnp.int32)

    def blk(a):
        return a.reshape(B, nb, Q_BLOCK, *a.shape[2:]).swapaxes(0, 1)

    def body(args):
        qb, qib, wib, pb = args
        return dsa_block(qb, qib, wib, pb, k, v, ki, pos, topk)

    o = lax.map(body, (blk(q), blk(qi), blk(wi), pos.reshape(nb, Q_BLOCK)))
    return o.swapaxes(0, 1).reshape(B, T, W_A)


def dsa_sample(q, qi, wi, k, v, ki, cache_k, cache_v, cache_ki):
    P = cache_k.shape[1]
    T = q.shape[1]
    k_all = jnp.concatenate([cache_k, k], axis=1)
    v_all = jnp.concatenate([cache_v, v], axis=1)
    ki_all = jnp.concatenate([cache_ki, ki], axis=1)
    kpos = jnp.arange(P + T, dtype=jnp.int32)
    qpos = P + jnp.arange(T, dtype=jnp.int32)
    topk = min(TOPK_MAX, (P + T) // 4)
    return dsa_block(q, qi, wi, qpos, k_all, v_all, ki_all, kpos, topk)


def rotate(x, pos):
    half = DK_B // 2
    inv = 1.0 / (ROT_BASE ** jnp.linspace(0.0, 1.0, half, dtype=jnp.float32))
    ang = pos.astype(jnp.float32)[:, None] * inv[None, :]
    cos = jnp.cos(ang)[None, :, None, :]
    sin = jnp.sin(ang)[None, :, None, :]
    xf = x.astype(jnp.float32)
    x1, x2 = xf[..., :half], xf[..., half:]
    return jnp.concatenate([x1 * cos - x2 * sin, x1 * sin + x2 * cos], axis=-1)


def retention_chunk(S, q, k, v, lg):
    C = q.shape[1]
    i = jnp.arange(C, dtype=jnp.float32)
    diff = i[:, None] - i[None, :]
    dmask = jnp.where(diff >= 0, jnp.exp(jnp.maximum(diff, 0.0)[None] * lg[:, None, None]), 0.0)
    qk = jnp.einsum('bihd,bjhd->bhij', q, k) * dmask[None]
    intra = jnp.einsum('bhij,bjhv->bihv', qk, v)
    inter = jnp.einsum('bihd,bhdv->bihv', q, S) * jnp.exp((i + 1.0)[:, None] * lg[None, :])[None, :, :, None]
    k_dec = k * jnp.exp((C - 1.0 - i)[:, None] * lg[None, :])[None, :, :, None]
    S_new = jnp.exp(C * lg)[None, :, None, None] * S + jnp.einsum('bjhd,bjhv->bhdv', k_dec, v)
    return intra + inter, S_new


def retention_mix(q_b, k_b, v_b, g_b, pos, S0, chunk):
    B, T = q_b.shape[:2]
    lg = jnp.log(1.0 - 2.0 ** (-5.0 - jnp.arange(H_B, dtype=jnp.float32)))
    q = rotate(q_b, pos)
    k = rotate(k_b, pos) * (DK_B ** -0.5)
    v = v_b.astype(jnp.float32)
    nc = T // chunk

    def to_chunks(a):
        return a.reshape(B, nc, chunk, *a.shape[2:]).swapaxes(0, 1)

    def step(S, inp):
        o, S_new = retention_chunk(S, inp[0], inp[1], inp[2], lg)
        return S_new, o

    S_fin, o = lax.scan(step, S0.astype(jnp.float32), (to_chunks(q), to_chunks(k), to_chunks(v)))
    o = o.swapaxes(0, 1).reshape(B, T, H_B, DV_B)
    mu = jnp.mean(o, axis=-1, keepdims=True)
    var = jnp.mean(jnp.square(o - mu), axis=-1, keepdims=True)
    o = ((o - mu) * lax.rsqrt(var + EPS)).reshape(B, T, W_B)
    return jax.nn.silu(g_b.astype(jnp.float32)) * o, S_fin


def peer(h, w_query, keys1, keys2, exp_u, exp_v):
    T = h.shape[0]
    q = (h @ w_query).reshape(T, PEER_HEADS, 2, D_PHALF).astype(jnp.float32)
    s1 = jnp.einsum('thd,nd->thn', q[:, :, 0], keys1.astype(jnp.float32))
    s2 = jnp.einsum('thd,nd->thn', q[:, :, 1], keys2.astype(jnp.float32))
    v1, i1 = lax.top_k(s1, PEER_TOPK)
    v2, i2 = lax.top_k(s2, PEER_TOPK)
    cand_s = (v1[..., :, None] + v2[..., None, :]).reshape(T, PEER_HEADS, PEER_TOPK * PEER_TOPK)
    cand_i = (i1[..., :, None] * N_KEYS + i2[..., None, :]).reshape(T, PEER_HEADS, PEER_TOPK * PEER_TOPK)
    top_s, sel = lax.top_k(cand_s, PEER_TOPK)
    e_idx = jnp.take_along_axis(cand_i, sel, axis=-1)
    gate = jax.nn.softmax(top_s, axis=-1)
    u = exp_u[e_idx]
    act = jax.nn.gelu(jnp.einsum('thkd,td->thk', u, h).astype(jnp.float32), approximate=False)
    return jnp.einsum('thk,thkd->td', (gate * act).astype(h.dtype), exp_v[e_idx])


def layer_prompt(x, ln1_g, w_in, w_out, ln2_g, pq, pk1, pk2, pu, pv):
    B, T, _ = x.shape
    pos = jnp.arange(T, dtype=jnp.int32)
    h = rmsnorm(x, ln1_g)
    q_a, k_a, v_a, q_i, k_i, w_i, q_b, k_b, v_b, g_b = project(h, w_in)
    o_a = dsa_prompt(q_a, q_i, w_i, k_a, v_a, k_i)
    S0 = jnp.zeros((B, H_B, DK_B, DV_B), jnp.float32)
    o_b, S_fin = retention_mix(q_b, k_b, v_b, g_b, pos, S0, CHUNK)
    x = x + jnp.concatenate([o_a, o_b.astype(x.dtype)], axis=-1) @ w_out
    h2 = rmsnorm(x, ln2_g)
    nb = (B * T) // TOK_BLOCK
    f = lax.map(lambda hb: peer(hb, pq, pk1, pk2, pu, pv), h2.reshape(nb, TOK_BLOCK, D_MODEL))
    x = x + f.reshape(B, T, D_MODEL)
    return x, k_a, v_a, k_i, S_fin.astype(x.dtype)


def layer_sample(x, ck, cv, cki, cs, ln1_g, w_in, w_out, ln2_g, pq, pk1, pk2, pu, pv):
    B, T, _ = x.shape
    P = ck.shape[1]
    pos = P + jnp.arange(T, dtype=jnp.int32)
    h = rmsnorm(x, ln1_g)
    q_a, k_a, v_a, q_i, k_i, w_i, q_b, k_b, v_b, g_b = project(h, w_in)
    o_a = dsa_sample(q_a, q_i, w_i, k_a, v_a, k_i, ck, cv, cki)
    o_b, S_new = retention_mix(q_b, k_b, v_b, g_b, pos, cs, T)
    x = x + jnp.concatenate([o_a, o_b.astype(x.dtype)], axis=-1) @ w_out
    h2 = rmsnorm(x, ln2_g)
    f = peer(h2.reshape(B * T, D_MODEL), pq, pk1, pk2, pu, pv)
    x = x + f.reshape(B, T, D_MODEL)
    return x, k_a, v_a, k_i, S_new.astype(cs.dtype)


def setup_inputs(seed: int = 0) -> dict:
    key = jax.random.key(seed)
    ks = jax.random.split(key, 16)

    def nrm(k, shape, s):
        return jax.random.normal(k, shape, jnp.float32) * s

    return {
        'x_prompt': nrm(ks[0], (BATCH, SEQ, D_MODEL), 1.0),
        'x_sample': nrm(ks[1], (DEC_BATCH, DEC_SEQ, D_MODEL), 1.0),
        'cache_attn_k': nrm(ks[2], (DEPTH, DEC_BATCH, PAST_LEN, KVH_A, DH_A), 1.0),
        'cache_attn_v': nrm(ks[3], (DEPTH, DEC_BATCH, PAST_LEN, KVH_A, DH_A), 1.0),
        'cache_idx_k': nrm(ks[4], (DEPTH, DEC_BATCH, PAST_LEN, D_IDX), 1.0),
        'state_retention': nrm(ks[5], (DEPTH, DEC_BATCH, H_B, DK_B, DV_B), 0.5),
        'ln1_g': 1.0 + nrm(ks[6], (DEPTH, D_MODEL), 0.02),
        'w_in': nrm(ks[7], (DEPTH, D_MODEL, D_IN), D_MODEL ** -0.5),
        'w_out': nrm(ks[8], (DEPTH, D_MIX, D_MODEL), D_MIX ** -0.5),
        'ln2_g': 1.0 + nrm(ks[9], (DEPTH, D_MODEL), 0.02),
        'peer_w_query': nrm(ks[10], (DEPTH, D_MODEL, PEER_HEADS * D_PKEY), D_MODEL ** -0.5),
        'peer_keys1': nrm(ks[11], (DEPTH, N_KEYS, D_PHALF), D_PHALF ** -0.5),
        'peer_keys2': nrm(ks[12], (DEPTH, N_KEYS, D_PHALF), D_PHALF ** -0.5),
        'peer_u': nrm(ks[13], (DEPTH, N_EXPERTS, D_MODEL), D_MODEL ** -0.5),
        'peer_v': nrm(ks[14], (DEPTH, N_EXPERTS, D_MODEL), PEER_HEADS ** -0.5),
        'ln_final_g': 1.0 + nrm(ks[15], (D_MODEL,), 0.02),
    }


def reference(x_prompt, x_sample, cache_attn_k, cache_attn_v, cache_idx_k, state_retention,
              ln1_g, w_in, w_out, ln2_g, peer_w_query, peer_keys1, peer_keys2, peer_u, peer_v, ln_final_g):
    xp = x_prompt
    xs = x_sample
    kp, vp, kip, sp = [], [], [], []
    ksm, vsm, kism, ssm = [], [], [], []
    for l in range(DEPTH):
        xp, k_a, v_a, k_i, S_p = layer_prompt(xp, ln1_g[l], w_in[l], w_out[l], ln2_g[l], peer_w_query[l],
                                              peer_keys1[l], peer_keys2[l], peer_u[l], peer_v[l])
        kp.append(k_a)
        vp.append(v_a)
        kip.append(k_i)
        sp.append(S_p)
        xs, k_s, v_s, ki_s, S_s = layer_sample(xs, cache_attn_k[l], cache_attn_v[l], cache_idx_k[l],
                                               state_retention[l], ln1_g[l], w_in[l], w_out[l], ln2_g[l],
                                               peer_w_query[l], peer_keys1[l], peer_keys2[l], peer_u[l], peer_v[l])
        ksm.append(k_s)
        vsm.append(v_s)
        kism.append(ki_s)
        ssm.append(S_s)
    y_prompt = rmsnorm(xp, ln_final_g)
    y_sample = rmsnorm(xs, ln_final_g)
    return (y_prompt, y_sample,
            jnp.stack(kp), jnp.stack(vp), jnp.stack(kip), jnp.stack(sp),
            jnp.stack(ksm), jnp.stack(vsm), jnp.stack(kism), jnp.stack(ssm))
```

```python
import functools

import jax
import jax.numpy as jnp
from jax import lax
from jax.experimental import pallas as pl
from jax.experimental.pallas import tpu as pltpu

F32 = jnp.float32
BF16 = jnp.bfloat16
I32 = jnp.int32

D_MODEL = 1024
CHUNK = 64
EPS = 1e-6
H_A, KVH_A, DH_A = 8, 2, 64
GRP_A = H_A // KVH_A
H_IDX, D_IDX = 8, 64
TOPK_MAX = 256
H_B, DK_B, DV_B = 4, 64, 128
ROT_BASE = 10000.0
W_A = H_A * DH_A
W_B = H_B * DV_B
N_KEYS = 128
PEER_HEADS = 8
PEER_TOPK = 16
D_PHALF = 128

INT_MIN = -(2 ** 31)
NEG_BIG = -1e30
MASKED_SCORE = -3.0e38
LANE = 128
VMEM_LIMIT_BYTES = 56 * 1024 * 1024

_NT = (((1,), (1,)), ((), ()))
_TN = (((0,), (0,)), ((), ()))


def _bdot(a, b, dims=None):
    a = a.astype(BF16)
    b = b.astype(BF16)
    if dims is None:
        return jnp.dot(a, b, preferred_element_type=F32)
    return lax.dot_general(a, b, dims, preferred_element_type=F32)


def _params(*sem):
    return pltpu.CompilerParams(dimension_semantics=sem, vmem_limit_bytes=VMEM_LIMIT_BYTES)


def _key_float(k):
    return pltpu.bitcast(k ^ ((k >> 31) & 0x7FFFFFFF), F32)


def _kth_largest_key(count_ge, k):
    kf = float(k)
    t0 = jnp.where(count_ge(0.0) >= kf, 0, INT_MIN).astype(I32)

    def bit_body(i, t):
        cand = t | lax.shift_left(jnp.int32(1), 30 - i)
        return jnp.where(count_ge(_key_float(cand)) >= kf, cand, t)

    return lax.fori_loop(0, 31, bit_body, t0)


def _inproj_kernel(x_ref, g_ref, wstd_ref, wt_ref,
                   ka_ref, va_ref, ki_ref, qb_ref, kb_ref, vb_ref, gb_ref,
                   qaT_ref, qiT_ref, vaT_ref, wiT_ref):
    x = x_ref[...]
    h = x * lax.rsqrt(jnp.mean(x * x, axis=-1, keepdims=True) + EPS) * g_ref[...]
    hb = h.astype(BF16)
    std = jnp.dot(hb, wstd_ref[...], preferred_element_type=F32)
    ka_ref[...] = std[:, 0:128]
    va_ref[...] = std[:, 128:256]
    ki_ref[...] = std[:, 256:320]
    qb_ref[...] = std[:, 384:640]
    kb_ref[...] = std[:, 640:896]
    vb_ref[...] = std[:, 896:1408]
    gb_ref[...] = std[:, 1408:1920]
    t = lax.dot_general(wt_ref[...], hb, _NT, preferred_element_type=F32)
    qaT_ref[...] = t[0:512]
    qiT_ref[...] = t[512:1024]
    vaT_ref[...] = t[1024:1152]
    wiT_ref[...] = t[1152:1160]


def _retention_perm():
    idx = []
    for half in range(2):
        for h in range(H_B):
            for d in range(DK_B // 2):
                idx.append(h * DK_B + half * (DK_B // 2) + d)
    return jnp.asarray(idx, dtype=I32)


def _prep_inproj_weights(w_in):
    o = [0]
    for s in (W_A, KVH_A * DH_A, KVH_A * DH_A, H_IDX * D_IDX, D_IDX, H_IDX,
              H_B * DK_B, H_B * DK_B, W_B, W_B):
        o.append(o[-1] + s)
    q_a, k_a, v_a, q_i, k_i, w_i, q_b, k_b, v_b, g_b = [w_in[:, o[i]:o[i + 1]] for i in range(10)]
    perm = _retention_perm()
    pad = jnp.zeros((D_MODEL, 64), w_in.dtype)
    wstd = jnp.concatenate([k_a, v_a, k_i, pad, q_b[:, perm], k_b[:, perm], v_b, g_b], axis=1)
    wt = jnp.concatenate([q_a, q_i, v_a, w_i], axis=1).T
    return wstd.astype(BF16), wt.astype(BF16)


def _inproj(x2d, ln_g, wstd, wt, tm):
    n = x2d.shape[0]
    assert n % tm == 0
    row = lambda c: pl.BlockSpec((tm, c), lambda i: (i, 0))
    col = lambda r: pl.BlockSpec((r, tm), lambda i: (0, i))
    full = lambda a: pl.BlockSpec(a.shape, lambda i: (0,) * a.ndim)
    g2 = ln_g.reshape(1, D_MODEL)
    out_shape = (
        jax.ShapeDtypeStruct((n, 128), F32), jax.ShapeDtypeStruct((n, 128), F32),
        jax.ShapeDtypeStruct((n, 64), F32),
        jax.ShapeDtypeStruct((n, 256), F32), jax.ShapeDtypeStruct((n, 256), F32),
        jax.ShapeDtypeStruct((n, 512), F32), jax.ShapeDtypeStruct((n, 512), F32),
        jax.ShapeDtypeStruct((512, n), F32), jax.ShapeDtypeStruct((512, n), F32),
        jax.ShapeDtypeStruct((128, n), F32), jax.ShapeDtypeStruct((8, n), F32),
    )
    out_specs = (row(128), row(128), row(64), row(256), row(256), row(512), row(512),
                 col(512), col(512), col(128), col(8))
    return pl.pallas_call(
        _inproj_kernel,
        grid=(n // tm,),
        in_specs=[row(D_MODEL), full(g2), full(wstd), full(wt)],
        out_specs=out_specs,
        out_shape=out_shape,
        compiler_params=_params("parallel"),
        name="inproj",
    )(x2d, g2, wstd, wt)


_KEY_ROWS = 128


def _dsa_body(qaT_ref, qiT_ref, wiT_ref, ka_ref, ki_ref, vaT_ref, oT_ref, score_ref, bias_ref,
              *, n_keys, tq, qpos0, n_valid, topk):
    ch = _KEY_ROWS
    nch = n_keys // ch
    row_iota = lax.broadcasted_iota(I32, (ch, tq), 0)
    lane_iota = lax.broadcasted_iota(I32, (ch, tq), 1)
    q_chunk = (qpos0 + lane_iota) // CHUNK
    wi = wiT_ref[...]

    def allowed(srow):
        return ((srow // CHUNK) <= q_chunk) & (srow < n_valid)

    qi = [qiT_ref[D_IDX * h:D_IDX * (h + 1), :].astype(BF16) for h in range(H_IDX)]

    def score_chunk(c, carry):
        r0 = pl.multiple_of(c * ch, ch)
        kic = ki_ref[pl.ds(r0, ch), :].astype(BF16)
        acc = jnp.zeros((ch, tq), F32)
        for h in range(H_IDX):
            d = jnp.dot(kic, qi[h], preferred_element_type=F32)
            acc = acc + wi[h:h + 1, :] * jnp.maximum(d, 0.0)
        score_ref[pl.ds(r0, ch), :] = jnp.where(allowed(r0 + row_iota), acc, MASKED_SCORE)
        return carry

    lax.fori_loop(0, nch, score_chunk, 0)

    def count(pred):
        def body(c, acc):
            r0 = pl.multiple_of(c * ch, ch)
            x = score_ref[pl.ds(r0, ch), :]
            hit = jnp.where(pred(x, r0 + row_iota), 1.0, 0.0)
            return acc + jnp.sum(hit.reshape(ch // 8, 8, tq), axis=0)
        acc = lax.fori_loop(0, nch, body, jnp.zeros((8, tq), F32))
        return jnp.sum(acc, axis=0, keepdims=True)

    t = _kth_largest_key(lambda piv: count(lambda x, r: x >= piv), topk)
    lo = _key_float(t)
    hi = _key_float(t + 1)
    need = float(topk) - count(lambda x, r: x >= hi)

    nbits = max(1, (n_keys - 1).bit_length())

    def idx_body(i, last):
        cand = last + lax.shift_left(jnp.int32(1), nbits - 1 - i)
        f = count(lambda x, r: (x >= lo) & (x < hi) & (r < cand))
        return jnp.where(f < need, cand, last)

    last = lax.fori_loop(0, nbits, idx_body, jnp.zeros((1, tq), I32))

    def bias_chunk(c, carry):
        r0 = pl.multiple_of(c * ch, ch)
        x = score_ref[pl.ds(r0, ch), :]
        srow = r0 + row_iota
        sel = ((x >= hi) | ((x >= lo) & (srow <= last))) & allowed(srow)
        bias_ref[pl.ds(r0, ch), :] = jnp.where(sel, 0.0, NEG_BIG)
        return carry

    lax.fori_loop(0, nch, bias_chunk, 0)

    zeros_half = jnp.zeros((DH_A, tq), F32)
    scale = DH_A ** -0.5
    for h in range(H_A):
        g = h // GRP_A
        qh = qaT_ref[DH_A * h:DH_A * (h + 1), :] * scale
        qpad = (jnp.concatenate([qh, zeros_half], axis=0) if g == 0
                else jnp.concatenate([zeros_half, qh], axis=0)).astype(BF16)

        def att_chunk(c, carry, qpad=qpad, g=g):
            m, l, acc = carry
            r0 = pl.multiple_of(c * ch, ch)
            kc = ka_ref[pl.ds(r0, ch), :].astype(BF16)
            s = jnp.dot(kc, qpad, preferred_element_type=F32) + bias_ref[pl.ds(r0, ch), :]
            m_new = jnp.maximum(m, jnp.max(s, axis=0, keepdims=True))
            alpha = jnp.exp(m - m_new)
            p = jnp.exp(s - m_new)
            l = l * alpha + jnp.sum(p, axis=0, keepdims=True)
            vc = vaT_ref[DH_A * g:DH_A * (g + 1), pl.ds(r0, ch)].astype(BF16)
            acc = acc * alpha + jnp.dot(vc, p.astype(BF16), preferred_element_type=F32)
            return m_new, l, acc

        m0 = jnp.full((1, tq), NEG_BIG, F32)
        l0 = jnp.zeros((1, tq), F32)
        a0 = jnp.zeros((DH_A, tq), F32)
        _, l, acc = lax.fori_loop(0, nch, att_chunk, (m0, l0, a0))
        oT_ref[DH_A * h:DH_A * (h + 1), :] = acc / l


def _dsa_kernel(qaT_ref, qiT_ref, wiT_ref, ka_ref, ki_ref, vaT_ref, oT_ref, score_ref, bias_ref,
                *, tq, variants, topk):
    j = pl.program_id(1)
    for jj, (n_keys, qpos0, n_valid) in enumerate(variants):
        @pl.when(j == jj)
        def _(n_keys=n_keys, qpos0=qpos0, n_valid=n_valid):
            _dsa_body(qaT_ref, qiT_ref, wiT_ref, ka_ref, ki_ref, vaT_ref, oT_ref, score_ref, bias_ref,
                      n_keys=n_keys, tq=tq, qpos0=qpos0, n_valid=n_valid, topk=topk)


def _dsa(qaT, qiT, wiT, ka, ki, vaT, *, nb, tq, variants, topk):
    nj = len(variants)
    s_max = ka.shape[1]
    qspec = lambda r: pl.BlockSpec((r, tq), lambda b, j: (0, b * nj + j))
    return pl.pallas_call(
        functools.partial(_dsa_kernel, tq=tq, variants=variants, topk=topk),
        grid=(nb, nj),
        in_specs=[qspec(512), qspec(512), qspec(8),
                  pl.BlockSpec((None, s_max, 128), lambda b, j: (b, 0, 0)),
                  pl.BlockSpec((None, s_max, 64), lambda b, j: (b, 0, 0)),
                  pl.BlockSpec((128, s_max), lambda b, j: (0, b))],
        out_specs=qspec(512),
        out_shape=jax.ShapeDtypeStruct((512, nb * nj * tq), F32),
        scratch_shapes=[pltpu.VMEM((s_max, tq), F32), pltpu.VMEM((s_max, tq), F32)],
        compiler_params=_params("parallel", "arbitrary"),
        name="dsa",
    )(qaT, qiT, wiT, ka, ki, vaT)


def _ret_kernel(q_ref, k_ref, v_ref, g_ref, cos_ref, sin_ref, dmask_ref, rdec_ref, kdec_ref,
                sdec_ref, s0_ref, o_ref, sout_ref, state_ref, *, c_len, nsub):
    ci = pl.program_id(1)

    @pl.when(ci == 0)
    def _():
        state_ref[...] = s0_ref[...]

    lane_head = (lax.broadcasted_iota(I32, (1, 2 * LANE), 1) % LANE) // (DK_B // 2)
    row_head = (lax.broadcasted_iota(I32, (2 * LANE, DV_B), 0) % LANE) // (DK_B // 2)
    for sub in range(nsub):
        rows = slice(sub * c_len, (sub + 1) * c_len)
        q = q_ref[rows, :]
        k = k_ref[rows, :]
        v = v_ref[rows, :]
        g = g_ref[rows, :]
        cos = cos_ref[rows, :]
        sin = sin_ref[rows, :]
        q1, q2 = q[:, :LANE], q[:, LANE:]
        k1, k2 = k[:, :LANE], k[:, LANE:]
        qr = jnp.concatenate([q1 * cos - q2 * sin, q1 * sin + q2 * cos], axis=1)
        kr = jnp.concatenate([k1 * cos - k2 * sin, k1 * sin + k2 * cos], axis=1) * (DK_B ** -0.5)
        state = state_ref[...]
        upd = _bdot(kr * kdec_ref[...], v, _TN)
        new_state = sdec_ref[...] * state
        vb = v.astype(BF16)
        krb = kr.astype(BF16)
        sb = state.astype(BF16)
        for h in range(H_B):
            qm = jnp.where(lane_head == h, qr, 0.0).astype(BF16)
            qk = lax.dot_general(qm, krb, _NT, preferred_element_type=F32) * dmask_ref[h]
            intra = jnp.dot(qk.astype(BF16), vb[:, DV_B * h:DV_B * (h + 1)], preferred_element_type=F32)
            inter = jnp.dot(qm, sb, preferred_element_type=F32) * rdec_ref[h]
            o = intra + inter
            mu = jnp.mean(o, axis=-1, keepdims=True)
            d = o - mu
            var = jnp.mean(d * d, axis=-1, keepdims=True)
            gh = g[:, DV_B * h:DV_B * (h + 1)]
            silu = gh * (1.0 / (1.0 + jnp.exp(-gh)))
            o_ref[rows, DV_B * h:DV_B * (h + 1)] = silu * (d * lax.rsqrt(var + EPS))
            new_state = new_state + jnp.where(row_head == h, upd[:, DV_B * h:DV_B * (h + 1)], 0.0)
        state_ref[...] = new_state

    @pl.when(ci == pl.num_programs(1) - 1)
    def _():
        sout_ref[...] = state_ref[...]


def _retention_tables(pos, c_len):
    half = DK_B // 2
    lg = jnp.log(1.0 - 2.0 ** (-5.0 - jnp.arange(H_B, dtype=F32)))
    inv = 1.0 / (ROT_BASE ** jnp.linspace(0.0, 1.0, half, dtype=F32))
    ang = pos.astype(F32)[:, None] * inv[None, :]
    cos = jnp.tile(jnp.cos(ang), (1, H_B))
    sin = jnp.tile(jnp.sin(ang), (1, H_B))
    i = jnp.arange(c_len, dtype=F32)
    diff = i[:, None] - i[None, :]
    dmask = jnp.where(diff >= 0, jnp.exp(jnp.maximum(diff, 0.0)[None] * lg[:, None, None]), 0.0)
    rdec = jnp.broadcast_to(jnp.exp((i + 1.0)[None, :] * lg[:, None])[:, :, None], (H_B, c_len, DV_B))
    lane_head = (jnp.arange(2 * LANE) % LANE) // half
    kdec = jnp.exp((c_len - 1.0 - i)[:, None] * lg[lane_head][None, :])
    sdec = jnp.broadcast_to(jnp.exp(c_len * lg[lane_head])[:, None], (2 * LANE, DV_B))
    return cos, sin, dmask.astype(F32), rdec.astype(F32), kdec.astype(F32), sdec.astype(F32)


def _state_to_packed(s):
    b = s.shape[0]
    return s.reshape(b, H_B, 2, DK_B // 2, DV_B).transpose(0, 2, 1, 3, 4).reshape(b, 2 * LANE, DV_B)


def _state_from_packed(s):
    b = s.shape[0]
    return s.reshape(b, 2, H_B, DK_B // 2, DV_B).transpose(0, 2, 1, 3, 4).reshape(b, H_B, DK_B, DV_B)


def _retention(qb, kb, vb, gb, s0_packed, pos, *, nb, t_len, c_len, nsub):
    blk = c_len * nsub
    nc = t_len // blk
    cos, sin, dmask, rdec, kdec, sdec = _retention_tables(pos, c_len)
    tok = lambda c: pl.BlockSpec((blk, c), lambda b, i: (b * nc + i, 0))
    tab = pl.BlockSpec((blk, LANE), lambda b, i: (i, 0))
    full = lambda a: pl.BlockSpec(a.shape, lambda b, i: (0,) * a.ndim)
    st = pl.BlockSpec((None, 2 * LANE, DV_B), lambda b, i: (b, 0, 0))
    return pl.pallas_call(
        functools.partial(_ret_kernel, c_len=c_len, nsub=nsub),
        grid=(nb, nc),
        in_specs=[tok(256), tok(256), tok(512), tok(512), tab, tab,
                  full(dmask), full(rdec), full(kdec), full(sdec), st],
        out_specs=(tok(512), st),
        out_shape=(jax.ShapeDtypeStruct((nb * t_len, W_B), F32),
                   jax.ShapeDtypeStruct((nb, 2 * LANE, DV_B), F32)),
        scratch_shapes=[pltpu.VMEM((2 * LANE, DV_B), F32)],
        compiler_params=_params("parallel", "arbitrary"),
        name="retention",
    )(qb, kb, vb, gb, cos, sin, dmask, rdec, kdec, sdec, s0_packed)


def _outproj_kernel(x_ref, oaT_ref, ob_ref, woa_ref, wob_ref, g2_ref, wq_ref, k1_ref, k2_ref,
                    x1_ref, h2T_ref, s1T_ref, s2T_ref):
    mix = _bdot(oaT_ref[...], woa_ref[...], _TN) + _bdot(ob_ref[...], wob_ref[...])
    x1 = x_ref[...] + mix
    x1_ref[...] = x1
    h2 = x1 * lax.rsqrt(jnp.mean(x1 * x1, axis=-1, keepdims=True) + EPS) * g2_ref[...]
    h2T_ref[...] = h2.T.astype(BF16)
    q = _bdot(h2, wq_ref[...])
    k1 = k1_ref[...]
    k2 = k2_ref[...]
    for h in range(PEER_HEADS):
        base = 2 * D_PHALF * h
        s1T_ref[h] = _bdot(k1, q[:, base:base + D_PHALF], _NT)
        s2T_ref[h] = _bdot(k2, q[:, base + D_PHALF:base + 2 * D_PHALF], _NT)


def _outproj(x2d, oaT, ob, woa, wob, ln2_g, wq, k1, k2, tm):
    n = x2d.shape[0]
    assert n % tm == 0
    g2 = ln2_g.reshape(1, D_MODEL)
    row = lambda c: pl.BlockSpec((tm, c), lambda i: (i, 0))
    col = lambda r: pl.BlockSpec((r, tm), lambda i: (0, i))
    full = lambda a: pl.BlockSpec(a.shape, lambda i: (0,) * a.ndim)
    sc = pl.BlockSpec((PEER_HEADS, N_KEYS, tm), lambda i: (0, 0, i))
    return pl.pallas_call(
        _outproj_kernel,
        grid=(n // tm,),
        in_specs=[row(D_MODEL), col(W_A), row(W_B), full(woa), full(wob), full(g2), full(wq),
                  full(k1), full(k2)],
        out_specs=(row(D_MODEL), col(D_MODEL), sc, sc),
        out_shape=(jax.ShapeDtypeStruct((n, D_MODEL), F32),
                   jax.ShapeDtypeStruct((D_MODEL, n), BF16),
                   jax.ShapeDtypeStruct((PEER_HEADS, N_KEYS, n), F32),
                   jax.ShapeDtypeStruct((PEER_HEADS, N_KEYS, n), F32)),
        compiler_params=_params("parallel"),
        name="outproj",
    )(x2d, oaT, ob, woa, wob, g2, wq, k1, k2)


def _peer_sel_kernel(s1T_ref, s2T_ref, thr_ref, a0_ref, b0_ref, zinv_ref, a_ref, b_ref):
    def top16(s, dst_ref):
        for r in range(PEER_TOPK):
            m = jnp.max(s, axis=0, keepdims=True)
            dst_ref[r:r + 1, :] = m
            s = jnp.where(s == m, -jnp.inf, s)

    def head(h, carry):
        top16(s1T_ref[h], a_ref)
        top16(s2T_ref[h], b_ref)
        a = a_ref[...]
        b = b_ref[...]
        a0 = a[0:1]
        b0 = b[0:1]
        ea = jnp.exp(a - a0)
        eb = jnp.exp(b - b0)
        cand = [a + b[0:1]] + [a[0:8] + b[j:j + 1] for j in range(1, 8)] + [a[0:1] + b[8:16]]
        prod = [ea * eb[0:1]] + [ea[0:8] * eb[j:j + 1] for j in range(1, 8)] + [ea[0:1] * eb[8:16]]
        cand = jnp.concatenate(cand, axis=0)
        prod = jnp.concatenate(prod, axis=0)
        t = _kth_largest_key(
            lambda piv: jnp.sum(jnp.where(cand >= piv, 1.0, 0.0), axis=0, keepdims=True), PEER_TOPK)
        thr = _key_float(t)
        z = jnp.sum(jnp.where(cand >= thr, prod, 0.0), axis=0, keepdims=True)
        thr_ref[h] = thr
        a0_ref[h] = a0
        b0_ref[h] = b0
        zinv_ref[h] = 1.0 / z
        return carry

    lax.fori_loop(0, PEER_HEADS, head, 0)


def _peer_select(s1T, s2T, tl):
    n = s1T.shape[2]
    assert n % tl == 0
    sc = pl.BlockSpec((PEER_HEADS, N_KEYS, tl), lambda i: (0, 0, i))
    st = pl.BlockSpec((PEER_HEADS, 1, tl), lambda i: (0, 0, i))
    stat = jax.ShapeDtypeStruct((PEER_HEADS, 1, n), F32)
    return pl.pallas_call(
        _peer_sel_kernel,
        grid=(n // tl,),
        in_specs=[sc, sc],
        out_specs=(st, st, st, st),
        out_shape=(stat, stat, stat, stat),
        scratch_shapes=[pltpu.VMEM((PEER_TOPK, tl), F32), pltpu.VMEM((PEER_TOPK, tl), F32)],
        compiler_params=_params("parallel"),
        name="peer_select",
    )(s1T, s2T)


_I1_PER_BLOCK = 8
_EXPERT_BLOCK = _I1_PER_BLOCK * N_KEYS
_SQRT_HALF = 0.7071067811865476


def _peer_dense_kernel(h2T_ref, u_ref, vT_ref, s1T_ref, s2T_ref, thr_ref, a0_ref, b0_ref, zinv_ref,
                       x1_ref, gf_ref, y_ref, e1_ref, e2_ref, aT_ref, wT_ref, acc_ref, *, tm):
    j = pl.program_id(1)
    n_i1 = _I1_PER_BLOCK
    n_lt = tm // LANE

    @pl.when(j == 0)
    def _():
        acc_ref[...] = jnp.zeros_like(acc_ref)
        for h in range(PEER_HEADS):
            e1_ref[h] = jnp.exp(s1T_ref[h] - a0_ref[h]) * zinv_ref[h]
            e2_ref[h] = jnp.exp(s2T_ref[h] - b0_ref[h])

    aT_ref[...] = jnp.dot(u_ref[...], h2T_ref[...], preferred_element_type=F32)

    i1_base = pl.multiple_of(j * n_i1, n_i1)
    for ii in range(n_i1):
        def sub_block(lt, carry, ii=ii):
            l0 = pl.multiple_of(lt * LANE, LANE)
            gate = jnp.zeros((N_KEYS, LANE), F32)
            for h in range(PEER_HEADS):
                s1 = s1T_ref[h, pl.ds(i1_base, n_i1), pl.ds(l0, LANE)][ii:ii + 1]
                e1 = e1_ref[h, pl.ds(i1_base, n_i1), pl.ds(l0, LANE)][ii:ii + 1]
                s2 = s2T_ref[h, :, pl.ds(l0, LANE)]
                e2 = e2_ref[h, :, pl.ds(l0, LANE)]
                thr = thr_ref[h, :, pl.ds(l0, LANE)]
                gate = gate + jnp.where(s1 + s2 >= thr, e1 * e2, 0.0)
            a = aT_ref[N_KEYS * ii:N_KEYS * (ii + 1), pl.ds(l0, LANE)]
            act = 0.5 * a * (1.0 + lax.erf(a * _SQRT_HALF))
            wT_ref[N_KEYS * ii:N_KEYS * (ii + 1), pl.ds(l0, LANE)] = (gate * act).astype(BF16)
            return carry

        lax.fori_loop(0, n_lt, sub_block, 0)
    acc_ref[...] += jnp.dot(vT_ref[...], wT_ref[...], preferred_element_type=F32)

    @pl.when(j == pl.num_programs(1) - 1)
    def _():
        x2 = x1_ref[...] + acc_ref[...].T
        y_ref[...] = x2 * lax.rsqrt(jnp.mean(x2 * x2, axis=-1, keepdims=True) + EPS) * gf_ref[...]


def _peer_dense(h2T, u_bf, vT_bf, s1T, s2T, thr, a0, b0, zinv, x1, lnf_g, tm):
    n = x1.shape[0]
    assert n % tm == 0
    ne = u_bf.shape[0] // _EXPERT_BLOCK
    gf = lnf_g.reshape(1, D_MODEL)
    sc = pl.BlockSpec((PEER_HEADS, N_KEYS, tm), lambda i, j: (0, 0, i))
    st = pl.BlockSpec((PEER_HEADS, 1, tm), lambda i, j: (0, 0, i))
    return pl.pallas_call(
        functools.partial(_peer_dense_kernel, tm=tm),
        grid=(n // tm, ne),
        in_specs=[pl.BlockSpec((D_MODEL, tm), lambda i, j: (0, i)),
                  pl.BlockSpec((_EXPERT_BLOCK, D_MODEL), lambda i, j: (j, 0)),
                  pl.BlockSpec((D_MODEL, _EXPERT_BLOCK), lambda i, j: (0, j)),
                  sc, sc, st, st, st, st,
                  pl.BlockSpec((tm, D_MODEL), lambda i, j: (i, 0)),
                  pl.BlockSpec((1, D_MODEL), lambda i, j: (0, 0))],
        out_specs=pl.BlockSpec((tm, D_MODEL), lambda i, j: (i, 0)),
        out_shape=jax.ShapeDtypeStruct((n, D_MODEL), F32),
        scratch_shapes=[pltpu.VMEM((PEER_HEADS, N_KEYS, tm), F32),
                        pltpu.VMEM((PEER_HEADS, N_KEYS, tm), F32),
                        pltpu.VMEM((_EXPERT_BLOCK, tm), F32),
                        pltpu.VMEM((_EXPERT_BLOCK, tm), BF16),
                        pltpu.VMEM((D_MODEL, tm), F32)],
        compiler_params=_params("parallel", "arbitrary"),
        name="peer_dense",
    )(h2T, u_bf, vT_bf, s1T, s2T, thr, a0, b0, zinv, x1, gf)


def _mix_to_output(x2d, oaT, ob, lw, tm_out, tl_sel, tm_dense):
    x1, h2T, s1T, s2T = _outproj(x2d, oaT, ob, lw["woa"], lw["wob"], lw["ln2_g"], lw["wq"],
                                 lw["k1"], lw["k2"], tm_out)
    thr, a0, b0, zinv = _peer_select(s1T, s2T, tl_sel)
    return _peer_dense(h2T, lw["u"], lw["vT"], s1T, s2T, thr, a0, b0, zinv, x1, lw["lnf_g"], tm_dense)


def _layer_prompt(x, lw):
    nb, t_len, _ = x.shape
    n = nb * t_len
    x2d = x.reshape(n, D_MODEL)
    ka, va, ki, qb, kb, vb, gb, qaT, qiT, vaT, wiT = _inproj(x2d, lw["ln1_g"], lw["wstd"], lw["wt"], 512)
    tq = 256
    nj = t_len // tq
    variants = tuple((tq * (j + 1), tq * j, tq * (j + 1)) for j in range(nj))
    oaT = _dsa(qaT, qiT, wiT, ka.reshape(nb, t_len, 128), ki.reshape(nb, t_len, 64), vaT,
               nb=nb, tq=tq, variants=variants, topk=min(TOPK_MAX, t_len // 4))
    s0 = jnp.zeros((nb, 2 * LANE, DV_B), F32)
    ob, s_fin = _retention(qb, kb, vb, gb, s0, jnp.arange(t_len, dtype=I32),
                           nb=nb, t_len=t_len, c_len=CHUNK, nsub=4)
    y = _mix_to_output(x2d, oaT, ob, lw, 512, 256, 512)
    return (y.reshape(nb, t_len, D_MODEL), ka.reshape(nb, t_len, KVH_A, DH_A),
            va.reshape(nb, t_len, KVH_A, DH_A), ki.reshape(nb, t_len, D_IDX), _state_from_packed(s_fin))


def _layer_sample(x, ck, cv, cki, cs, lw):
    nb, t_len, _ = x.shape
    past = ck.shape[1]
    n = nb * t_len
    x2d = x.reshape(n, D_MODEL)
    ka, va, ki, qb, kb, vb, gb, qaT, qiT, vaT, wiT = _inproj(x2d, lw["ln1_g"], lw["wstd"], lw["wt"], n)
    total = past + t_len
    s_pad = -(-total // LANE) * LANE
    padk = lambda a: jnp.pad(a, ((0, 0), (0, s_pad - total), (0, 0)))
    k_all = padk(jnp.concatenate([ck.reshape(nb, past, 128), ka.reshape(nb, t_len, 128)], axis=1))
    v_all = padk(jnp.concatenate([cv.reshape(nb, past, 128), va.reshape(nb, t_len, 128)], axis=1))
    ki_all = padk(jnp.concatenate([cki, ki.reshape(nb, t_len, D_IDX)], axis=1))
    vaT_all = v_all.transpose(2, 0, 1).reshape(128, nb * s_pad)
    tq = LANE
    padq = lambda a: jnp.pad(a.reshape(a.shape[0], nb, t_len),
                             ((0, 0), (0, 0), (0, tq - t_len))).reshape(a.shape[0], nb * tq)
    oaT = _dsa(padq(qaT), padq(qiT), padq(wiT), k_all, ki_all, vaT_all,
               nb=nb, tq=tq, variants=((s_pad, past, total),), topk=min(TOPK_MAX, total // 4))
    oaT = oaT.reshape(W_A, nb, tq)[:, :, :t_len].reshape(W_A, n)
    ob, s_new = _retention(qb, kb, vb, gb, _state_to_packed(cs), past + jnp.arange(t_len, dtype=I32),
                           nb=nb, t_len=t_len, c_len=t_len, nsub=1)
    y = _mix_to_output(x2d, oaT, ob, lw, n, n, n)
    return (y.reshape(nb, t_len, D_MODEL), ka.reshape(nb, t_len, KVH_A, DH_A),
            va.reshape(nb, t_len, KVH_A, DH_A), ki.reshape(nb, t_len, D_IDX), _state_from_packed(s_new))


def kernel(x_prompt, x_sample, cache_attn_k, cache_attn_v, cache_idx_k, state_retention,
           ln1_g, w_in, w_out, ln2_g, peer_w_query, peer_keys1, peer_keys2, peer_u, peer_v, ln_final_g):
    depth = w_in.shape[0]
    assert depth == 1, "the final norm is fused into the (single) layer's last kernel"
    wstd, wt = _prep_inproj_weights(w_in[0])
    lw = dict(ln1_g=ln1_g[0], wstd=wstd, wt=wt,
              woa=w_out[0, :W_A].astype(BF16), wob=w_out[0, W_A:].astype(BF16),
              ln2_g=ln2_g[0], wq=peer_w_query[0].astype(BF16),
              k1=peer_keys1[0], k2=peer_keys2[0],
              u=peer_u[0].astype(BF16), vT=peer_v[0].T.astype(BF16), lnf_g=ln_final_g)
    yp, kp, vp, kip, sp = _layer_prompt(x_prompt, lw)
    ys, ks, vs, kis, ss = _layer_sample(x_sample, cache_attn_k[0], cache_attn_v[0], cache_idx_k[0],
                                        state_retention[0], lw)
    st = lambda a: a[None]
    return (yp, ys, st(kp), st(vp), st(kip), st(sp), st(ks), st(vs), st(kis), st(ss))
```

```python
import functools

import jax
import jax.numpy as jnp
from jax import lax
from jax.experimental import pallas as pl
from jax.experimental.pallas import tpu as pltpu

F32 = jnp.float32
BF16 = jnp.bfloat16
I32 = jnp.int32

D_MODEL = 1024
CHUNK = 64
EPS = 1e-6
H_A, KVH_A, DH_A = 8, 2, 64
GRP_A = H_A // KVH_A
H_IDX, D_IDX = 8, 64
TOPK_MAX = 256
H_B, DK_B, DV_B = 4, 64, 128
ROT_BASE = 10000.0
W_A = H_A * DH_A
W_B = H_B * DV_B
N_KEYS = 128
PEER_HEADS = 8
PEER_TOPK = 16
D_PHALF = 128

INT_MIN = -(2 ** 31)
NEG_BIG = -1e30
MASKED_SCORE = -3.0e38
LANE = 128
VMEM_LIMIT_BYTES = 56 * 1024 * 1024

_NT = (((1,), (1,)), ((), ()))
_TN = (((0,), (0,)), ((), ()))


def _bdot(a, b, dims=None):
    a = a.astype(BF16)
    b = b.astype(BF16)
    if dims is None:
        return jnp.dot(a, b, preferred_element_type=F32)
    return lax.dot_general(a, b, dims, preferred_element_type=F32)


def _params(*sem):
    return pltpu.CompilerParams(dimension_semantics=sem, vmem_limit_bytes=VMEM_LIMIT_BYTES)


def _key_float(k):
    return pltpu.bitcast(k ^ ((k >> 31) & 0x7FFFFFFF), F32)


def _kth_largest_key(count_ge, k):
    kf = float(k)
    t0 = jnp.where(count_ge(0.0) >= kf, 0, INT_MIN).astype(I32)

    def bit_body(i, t):
        cand = t | lax.shift_left(jnp.int32(1), 30 - i)
        return jnp.where(count_ge(_key_float(cand)) >= kf, cand, t)

    return lax.fori_loop(0, 31, bit_body, t0)


def _inproj_kernel(x_ref, g_ref, wstd_ref, wt_ref,
                   ka_ref, va_ref, ki_ref, qb_ref, kb_ref, vb_ref, gb_ref,
                   qaT_ref, qiT_ref, vaT_ref, wiT_ref):
    x = x_ref[...]
    h = x * lax.rsqrt(jnp.mean(x * x, axis=-1, keepdims=True) + EPS) * g_ref[...]
    hb = h.astype(BF16)
    std = jnp.dot(hb, wstd_ref[...], preferred_element_type=F32)
    ka_ref[...] = std[:, 0:128]
    va_ref[...] = std[:, 128:256]
    ki_ref[...] = std[:, 256:320]
    qb_ref[...] = std[:, 384:640]
    kb_ref[...] = std[:, 640:896]
    vb_ref[...] = std[:, 896:1408]
    gb_ref[...] = std[:, 1408:1920]
    t = lax.dot_general(wt_ref[...], hb, _NT, preferred_element_type=F32)
    qaT_ref[...] = t[0:512]
    qiT_ref[...] = t[512:1024]
    vaT_ref[...] = t[1024:1152]
    wiT_ref[...] = t[1152:1160]


def _retention_perm():
    idx = []
    for half in range(2):
        for h in range(H_B):
            for d in range(DK_B // 2):
                idx.append(h * DK_B + half * (DK_B // 2) + d)
    return jnp.asarray(idx, dtype=I32)


def _prep_inproj_weights(w_in):
    o = [0]
    for s in (W_A, KVH_A * DH_A, KVH_A * DH_A, H_IDX * D_IDX, D_IDX, H_IDX,
              H_B * DK_B, H_B * DK_B, W_B, W_B):
        o.append(o[-1] + s)
    q_a, k_a, v_a, q_i, k_i, w_i, q_b, k_b, v_b, g_b = [w_in[:, o[i]:o[i + 1]] for i in range(10)]
    perm = _retention_perm()
    pad = jnp.zeros((D_MODEL, 64), w_in.dtype)
    wstd = jnp.concatenate([k_a, v_a, k_i, pad, q_b[:, perm], k_b[:, perm], v_b, g_b], axis=1)
    wt = jnp.concatenate([q_a, q_i, v_a, w_i], axis=1).T
    return wstd.astype(BF16), wt.astype(BF16)


def _inproj(x2d, ln_g, wstd, wt, tm):
    n = x2d.shape[0]
    assert n % tm == 0
    row = lambda c: pl.BlockSpec((tm, c), lambda i: (i, 0))
    col = lambda r: pl.BlockSpec((r, tm), lambda i: (0, i))
    full = lambda a: pl.BlockSpec(a.shape, lambda i: (0,) * a.ndim)
    g2 = ln_g.reshape(1, D_MODEL)
    out_shape = (
        jax.ShapeDtypeStruct((n, 128), F32), jax.ShapeDtypeStruct((n, 128), F32),
        jax.ShapeDtypeStruct((n, 64), F32),
        jax.ShapeDtypeStruct((n, 256), F32), jax.ShapeDtypeStruct((n, 256), F32),
        jax.ShapeDtypeStruct((n, 512), F32), jax.ShapeDtypeStruct((n, 512), F32),
        jax.ShapeDtypeStruct((512, n), F32), jax.ShapeDtypeStruct((512, n), F32),
        jax.ShapeDtypeStruct((128, n), F32), jax.ShapeDtypeStruct((8, n), F32),
    )
    out_specs = (row(128), row(128), row(64), row(256), row(256), row(512), row(512),
                 col(512), col(512), col(128), col(8))
    return pl.pallas_call(
        _inproj_kernel,
        grid=(n // tm,),
        in_specs=[row(D_MODEL), full(g2), full(wstd), full(wt)],
        out_specs=out_specs,
        out_shape=out_shape,
        compiler_params=_params("parallel"),
        name="inproj",
    )(x2d, g2, wstd, wt)


_KEY_ROWS = 128


def _dsa_body(qaT_ref, qiT_ref, wiT_ref, ka_ref, ki_ref, vaT_ref, oT_ref, score_ref, bias_ref,
              qpad_ref, s_ref, p_ref, m_ref, l_ref,
              *, n_keys, tq, qpos0, n_valid, topk):
    ch = _KEY_ROWS
    nch = n_keys // ch
    row_iota = lax.broadcasted_iota(I32, (ch, tq), 0)
    lane_iota = lax.broadcasted_iota(I32, (ch, tq), 1)
    q_chunk = (qpos0 + lane_iota) // CHUNK
    wi = wiT_ref[...]

    def allowed(srow):
        return ((srow // CHUNK) <= q_chunk) & (srow < n_valid)

    qi = [qiT_ref[D_IDX * h:D_IDX * (h + 1), :].astype(BF16) for h in range(H_IDX)]

    def score_chunk(c, carry):
        r0 = pl.multiple_of(c * ch, ch)
        kic = ki_ref[pl.ds(r0, ch), :].astype(BF16)
        acc = jnp.zeros((ch, tq), F32)
        for h in range(H_IDX):
            d = jnp.dot(kic, qi[h], preferred_element_type=F32)
            acc = acc + wi[h:h + 1, :] * jnp.maximum(d, 0.0)
        score_ref[pl.ds(r0, ch), :] = jnp.where(allowed(r0 + row_iota), acc, MASKED_SCORE)
        return carry

    lax.fori_loop(0, nch, score_chunk, 0)

    def count(pred):
        def body(c, acc):
            r0 = pl.multiple_of(c * ch, ch)
            x = score_ref[pl.ds(r0, ch), :]
            hit = jnp.where(pred(x, r0 + row_iota), 1.0, 0.0)
            return acc + jnp.sum(hit.reshape(ch // 8, 8, tq), axis=0)
        acc = lax.fori_loop(0, nch, body, jnp.zeros((8, tq), F32), unroll=min(nch, 4))
        return jnp.sum(acc, axis=0, keepdims=True)

    t = _kth_largest_key(lambda piv: count(lambda x, r: x >= piv), topk)
    lo = _key_float(t)
    hi = _key_float(t + 1)
    need = float(topk) - count(lambda x, r: x >= hi)

    tri = jnp.where(lax.broadcasted_iota(I32, (ch, ch), 0) > lax.broadcasted_iota(I32, (ch, ch), 1),
                    1.0, 0.0).astype(BF16)

    def bias_chunk(c, ties_before):
        r0 = pl.multiple_of(c * ch, ch)
        x = score_ref[pl.ds(r0, ch), :]
        tie = (x >= lo) & (x < hi)
        tie_f = jnp.where(tie, 1.0, 0.0)
        before = ties_before + jnp.dot(tri, tie_f.astype(BF16), preferred_element_type=F32)
        sel = ((x >= hi) | (tie & (before < need))) & allowed(r0 + row_iota)
        bias_ref[pl.ds(r0, ch), :] = jnp.where(sel, 0.0, NEG_BIG)
        return ties_before + jnp.sum(tie_f, axis=0, keepdims=True)

    lax.fori_loop(0, nch, bias_chunk, jnp.zeros((1, tq), F32))

    zeros_half = jnp.zeros((DH_A, tq), F32)
    scale = DH_A ** -0.5
    for h in range(H_A):
        qh = qaT_ref[DH_A * h:DH_A * (h + 1), :] * scale
        qpad = (jnp.concatenate([qh, zeros_half], axis=0) if h // GRP_A == 0
                else jnp.concatenate([zeros_half, qh], axis=0))
        qpad_ref[:, tq * h:tq * (h + 1)] = qpad.astype(BF16)
    m_ref[...] = jnp.full(m_ref.shape, NEG_BIG, F32)
    l_ref[...] = jnp.zeros(l_ref.shape, F32)
    oT_ref[...] = jnp.zeros(oT_ref.shape, F32)

    def att_chunk(c, carry):
        r0 = pl.multiple_of(c * ch, ch)
        kc = ka_ref[pl.ds(r0, ch), :].astype(BF16)
        bias = bias_ref[pl.ds(r0, ch), :]
        s_ref[...] = jnp.dot(kc, qpad_ref[...], preferred_element_type=F32)
        alphas = []
        for h in range(H_A):
            lanes = slice(tq * h, tq * (h + 1))
            s = s_ref[:, lanes] + bias
            m_old = m_ref[h:h + 1, :]
            m_new = jnp.maximum(m_old, jnp.max(s, axis=0, keepdims=True))
            alpha = jnp.exp(m_old - m_new)
            p = jnp.exp(s - m_new)
            l_ref[h:h + 1, :] = l_ref[h:h + 1, :] * alpha + jnp.sum(p, axis=0, keepdims=True)
            p_ref[:, lanes] = p.astype(BF16)
            m_ref[h:h + 1, :] = m_new
            alphas.append(alpha)
        for g in range(KVH_A):
            vc = vaT_ref[DH_A * g:DH_A * (g + 1), pl.ds(r0, ch)].astype(BF16)
            pv = jnp.dot(vc, p_ref[:, GRP_A * tq * g:GRP_A * tq * (g + 1)],
                         preferred_element_type=F32)
            for hh in range(GRP_A):
                h = GRP_A * g + hh
                rows = slice(DH_A * h, DH_A * (h + 1))
                oT_ref[rows, :] = oT_ref[rows, :] * alphas[h] + pv[:, tq * hh:tq * (hh + 1)]
        return carry

    lax.fori_loop(0, nch, att_chunk, 0)
    for h in range(H_A):
        rows = slice(DH_A * h, DH_A * (h + 1))
        oT_ref[rows, :] = oT_ref[rows, :] / l_ref[h:h + 1, :]


def _dsa_kernel(qaT_ref, qiT_ref, wiT_ref, ka_ref, ki_ref, vaT_ref, oT_ref, score_ref, bias_ref,
              qpad_ref, s_ref, p_ref, m_ref, l_ref,
                *, tq, variants, topk):
    j = pl.program_id(1)
    for jj, (n_keys, qpos0, n_valid) in enumerate(variants):
        @pl.when(j == jj)
        def _(n_keys=n_keys, qpos0=qpos0, n_valid=n_valid):
            _dsa_body(qaT_ref, qiT_ref, wiT_ref, ka_ref, ki_ref, vaT_ref, oT_ref, score_ref, bias_ref,
              qpad_ref, s_ref, p_ref, m_ref, l_ref,
                      n_keys=n_keys, tq=tq, qpos0=qpos0, n_valid=n_valid, topk=topk)


def _dsa(qaT, qiT, wiT, ka, ki, vaT, *, nb, tq, variants, topk):
    nj = len(variants)
    s_max = ka.shape[1]
    qspec = lambda r: pl.BlockSpec((r, tq), lambda b, j: (0, b * nj + j))
    return pl.pallas_call(
        functools.partial(_dsa_kernel, tq=tq, variants=variants, topk=topk),
        grid=(nb, nj),
        in_specs=[qspec(512), qspec(512), qspec(8),
                  pl.BlockSpec((None, s_max, 128), lambda b, j: (b, 0, 0)),
                  pl.BlockSpec((None, s_max, 64), lambda b, j: (b, 0, 0)),
                  pl.BlockSpec((128, s_max), lambda b, j: (0, b))],
        out_specs=qspec(512),
        out_shape=jax.ShapeDtypeStruct((512, nb * nj * tq), F32),
        scratch_shapes=[pltpu.VMEM((s_max, tq), F32), pltpu.VMEM((s_max, tq), F32),
                        pltpu.VMEM((KVH_A * DH_A, H_A * tq), BF16),
                        pltpu.VMEM((_KEY_ROWS, H_A * tq), F32),
                        pltpu.VMEM((_KEY_ROWS, H_A * tq), BF16),
                        pltpu.VMEM((H_A, tq), F32), pltpu.VMEM((H_A, tq), F32)],
        compiler_params=_params("parallel", "arbitrary"),
        name="dsa",
    )(qaT, qiT, wiT, ka, ki, vaT)


def _ret_kernel(q_ref, k_ref, v_ref, g_ref, cos_ref, sin_ref, dmask_ref, rdec_ref, kdec_ref,
                sdec_ref, s0_ref, o_ref, sout_ref, state_ref, *, c_len, nsub):
    ci = pl.program_id(1)

    @pl.when(ci == 0)
    def _():
        state_ref[...] = s0_ref[...]

    lane_head = (lax.broadcasted_iota(I32, (1, 2 * LANE), 1) % LANE) // (DK_B // 2)
    row_head = (lax.broadcasted_iota(I32, (2 * LANE, DV_B), 0) % LANE) // (DK_B // 2)
    for sub in range(nsub):
        rows = slice(sub * c_len, (sub + 1) * c_len)
        q = q_ref[rows, :]
        k = k_ref[rows, :]
        v = v_ref[rows, :]
        g = g_ref[rows, :]
        cos = cos_ref[rows, :]
        sin = sin_ref[rows, :]
        q1, q2 = q[:, :LANE], q[:, LANE:]
        k1, k2 = k[:, :LANE], k[:, LANE:]
        qr = jnp.concatenate([q1 * cos - q2 * sin, q1 * sin + q2 * cos], axis=1)
        kr = jnp.concatenate([k1 * cos - k2 * sin, k1 * sin + k2 * cos], axis=1) * (DK_B ** -0.5)
        state = state_ref[...]
        upd = _bdot(kr * kdec_ref[...], v, _TN)
        new_state = sdec_ref[...] * state
        vb = v.astype(BF16)
        krb = kr.astype(BF16)
        sb = state.astype(BF16)
        for h in range(H_B):
            qm = jnp.where(lane_head == h, qr, 0.0).astype(BF16)
            qk = lax.dot_general(qm, krb, _NT, preferred_element_type=F32) * dmask_ref[h]
            intra = jnp.dot(qk.astype(BF16), vb[:, DV_B * h:DV_B * (h + 1)], preferred_element_type=F32)
            inter = jnp.dot(qm, sb, preferred_element_type=F32) * rdec_ref[h]
            o = intra + inter
            mu = jnp.mean(o, axis=-1, keepdims=True)
            d = o - mu
            var = jnp.mean(d * d, axis=-1, keepdims=True)
            gh = g[:, DV_B * h:DV_B * (h + 1)]
            silu = gh * (1.0 / (1.0 + jnp.exp(-gh)))
            o_ref[rows, DV_B * h:DV_B * (h + 1)] = silu * (d * lax.rsqrt(var + EPS))
            new_state = new_state + jnp.where(row_head == h, upd[:, DV_B * h:DV_B * (h + 1)], 0.0)
        state_ref[...] = new_state

    @pl.when(ci == pl.num_programs(1) - 1)
    def _():
        sout_ref[...] = state_ref[...]


def _retention_tables(pos, c_len):
    half = DK_B // 2
    lg = jnp.log(1.0 - 2.0 ** (-5.0 - jnp.arange(H_B, dtype=F32)))
    inv = 1.0 / (ROT_BASE ** jnp.linspace(0.0, 1.0, half, dtype=F32))
    ang = pos.astype(F32)[:, None] * inv[None, :]
    cos = jnp.tile(jnp.cos(ang), (1, H_B))
    sin = jnp.tile(jnp.sin(ang), (1, H_B))
    i = jnp.arange(c_len, dtype=F32)
    diff = i[:, None] - i[None, :]
    dmask = jnp.where(diff >= 0, jnp.exp(jnp.maximum(diff, 0.0)[None] * lg[:, None, None]), 0.0)
    rdec = jnp.broadcast_to(jnp.exp((i + 1.0)[None, :] * lg[:, None])[:, :, None], (H_B, c_len, DV_B))
    lane_head = (jnp.arange(2 * LANE) % LANE) // half
    kdec = jnp.exp((c_len - 1.0 - i)[:, None] * lg[lane_head][None, :])
    sdec = jnp.broadcast_to(jnp.exp(c_len * lg[lane_head])[:, None], (2 * LANE, DV_B))
    return cos, sin, dmask.astype(F32), rdec.astype(F32), kdec.astype(F32), sdec.astype(F32)


def _state_to_packed(s):
    b = s.shape[0]
    return s.reshape(b, H_B, 2, DK_B // 2, DV_B).transpose(0, 2, 1, 3, 4).reshape(b, 2 * LANE, DV_B)


def _state_from_packed(s):
    b = s.shape[0]
    return s.reshape(b, 2, H_B, DK_B // 2, DV_B).transpose(0, 2, 1, 3, 4).reshape(b, H_B, DK_B, DV_B)


def _retention(qb, kb, vb, gb, s0_packed, pos, *, nb, t_len, c_len, nsub):
    blk = c_len * nsub
    nc = t_len // blk
    cos, sin, dmask, rdec, kdec, sdec = _retention_tables(pos, c_len)
    tok = lambda c: pl.BlockSpec((blk, c), lambda b, i: (b * nc + i, 0))
    tab = pl.BlockSpec((blk, LANE), lambda b, i: (i, 0))
    full = lambda a: pl.BlockSpec(a.shape, lambda b, i: (0,) * a.ndim)
    st = pl.BlockSpec((None, 2 * LANE, DV_B), lambda b, i: (b, 0, 0))
    return pl.pallas_call(
        functools.partial(_ret_kernel, c_len=c_len, nsub=nsub),
        grid=(nb, nc),
        in_specs=[tok(256), tok(256), tok(512), tok(512), tab, tab,
                  full(dmask), full(rdec), full(kdec), full(sdec), st],
        out_specs=(tok(512), st),
        out_shape=(jax.ShapeDtypeStruct((nb * t_len, W_B), F32),
                   jax.ShapeDtypeStruct((nb, 2 * LANE, DV_B), F32)),
        scratch_shapes=[pltpu.VMEM((2 * LANE, DV_B), F32)],
        compiler_params=_params("parallel", "arbitrary"),
        name="retention",
    )(qb, kb, vb, gb, cos, sin, dmask, rdec, kdec, sdec, s0_packed)


def _outproj_kernel(x_ref, oaT_ref, ob_ref, woa_ref, wob_ref, g2_ref, wq_ref, k1_ref, k2_ref,
                    x1_ref, h2T_ref, s1T_ref, s2T_ref):
    mix = _bdot(oaT_ref[...], woa_ref[...], _TN) + _bdot(ob_ref[...], wob_ref[...])
    x1 = x_ref[...] + mix
    x1_ref[...] = x1
    h2 = x1 * lax.rsqrt(jnp.mean(x1 * x1, axis=-1, keepdims=True) + EPS) * g2_ref[...]
    h2T_ref[...] = h2.T.astype(BF16)
    q = _bdot(h2, wq_ref[...])
    k1 = k1_ref[...]
    k2 = k2_ref[...]
    for h in range(PEER_HEADS):
        base = 2 * D_PHALF * h
        s1T_ref[h] = _bdot(k1, q[:, base:base + D_PHALF], _NT)
        s2T_ref[h] = _bdot(k2, q[:, base + D_PHALF:base + 2 * D_PHALF], _NT)


def _outproj(x2d, oaT, ob, woa, wob, ln2_g, wq, k1, k2, tm):
    n = x2d.shape[0]
    assert n % tm == 0
    g2 = ln2_g.reshape(1, D_MODEL)
    row = lambda c: pl.BlockSpec((tm, c), lambda i: (i, 0))
    col = lambda r: pl.BlockSpec((r, tm), lambda i: (0, i))
    full = lambda a: pl.BlockSpec(a.shape, lambda i: (0,) * a.ndim)
    sc = pl.BlockSpec((PEER_HEADS, N_KEYS, tm), lambda i: (0, 0, i))
    return pl.pallas_call(
        _outproj_kernel,
        grid=(n // tm,),
        in_specs=[row(D_MODEL), col(W_A), row(W_B), full(woa), full(wob), full(g2), full(wq),
                  full(k1), full(k2)],
        out_specs=(row(D_MODEL), col(D_MODEL), sc, sc),
        out_shape=(jax.ShapeDtypeStruct((n, D_MODEL), F32),
                   jax.ShapeDtypeStruct((D_MODEL, n), BF16),
                   jax.ShapeDtypeStruct((PEER_HEADS, N_KEYS, n), F32),
                   jax.ShapeDtypeStruct((PEER_HEADS, N_KEYS, n), F32)),
        compiler_params=_params("parallel"),
        name="outproj",
    )(x2d, oaT, ob, woa, wob, g2, wq, k1, k2)


def _peer_sel_kernel(s1T_ref, s2T_ref, thr_ref, a0_ref, b0_ref, zinv_ref, a_ref, b_ref):
    def top16(s, dst_ref):
        for r in range(PEER_TOPK):
            m = jnp.max(s, axis=0, keepdims=True)
            dst_ref[r:r + 1, :] = m
            s = jnp.where(s == m, -jnp.inf, s)

    def head(h, carry):
        top16(s1T_ref[h], a_ref)
        top16(s2T_ref[h], b_ref)
        a = a_ref[...]
        b = b_ref[...]
        a0 = a[0:1]
        b0 = b[0:1]
        ea = jnp.exp(a - a0)
        eb = jnp.exp(b - b0)
        cand = [a + b[0:1]] + [a[0:8] + b[j:j + 1] for j in range(1, 8)] + [a[0:1] + b[8:16]]
        prod = [ea * eb[0:1]] + [ea[0:8] * eb[j:j + 1] for j in range(1, 8)] + [ea[0:1] * eb[8:16]]
        cand = jnp.concatenate(cand, axis=0)
        prod = jnp.concatenate(prod, axis=0)
        t = _kth_largest_key(
            lambda piv: jnp.sum(jnp.where(cand >= piv, 1.0, 0.0), axis=0, keepdims=True), PEER_TOPK)
        thr = _key_float(t)
        z = jnp.sum(jnp.where(cand >= thr, prod, 0.0), axis=0, keepdims=True)
        thr_ref[h] = thr
        a0_ref[h] = a0
        b0_ref[h] = b0
        zinv_ref[h] = 1.0 / z
        return carry

    lax.fori_loop(0, PEER_HEADS, head, 0)


def _peer_select(s1T, s2T, tl):
    n = s1T.shape[2]
    assert n % tl == 0
    sc = pl.BlockSpec((PEER_HEADS, N_KEYS, tl), lambda i: (0, 0, i))
    st = pl.BlockSpec((PEER_HEADS, 1, tl), lambda i: (0, 0, i))
    stat = jax.ShapeDtypeStruct((PEER_HEADS, 1, n), F32)
    return pl.pallas_call(
        _peer_sel_kernel,
        grid=(n // tl,),
        in_specs=[sc, sc],
        out_specs=(st, st, st, st),
        out_shape=(stat, stat, stat, stat),
        scratch_shapes=[pltpu.VMEM((PEER_TOPK, tl), F32), pltpu.VMEM((PEER_TOPK, tl), F32)],
        compiler_params=_params("parallel"),
        name="peer_select",
    )(s1T, s2T)


_I1_PER_BLOCK = 8
_EXPERT_BLOCK = _I1_PER_BLOCK * N_KEYS
_SQRT_HALF = 0.7071067811865476


def _peer_dense_kernel(h2T_ref, u_ref, vT_ref, s1T_ref, s2T_ref, thr_ref, a0_ref, b0_ref, zinv_ref,
                       x1_ref, gf_ref, y_ref, e1_ref, e2_ref, wT_ref, acc_ref, *, tm):
    j = pl.program_id(1)
    n_i1 = _I1_PER_BLOCK
    n_lt = tm // LANE

    @pl.when(j == 0)
    def _():
        acc_ref[...] = jnp.zeros_like(acc_ref)
        for h in range(PEER_HEADS):
            e1_ref[h] = jnp.exp(s1T_ref[h] - a0_ref[h]) * zinv_ref[h]
            e2_ref[h] = jnp.exp(s2T_ref[h] - b0_ref[h])

    i1_base = pl.multiple_of(j * n_i1, n_i1)
    n_half = 2 if n_lt % 2 == 0 else 1
    lt_per_half = n_lt // n_half
    for hf in range(n_half):
        lanes_hf = slice(hf * lt_per_half * LANE, (hf + 1) * lt_per_half * LANE)
        for ii in range(n_i1):
            rows = slice(N_KEYS * ii, N_KEYS * (ii + 1))
            a_blk = jnp.dot(u_ref[rows, :], h2T_ref[:, lanes_hf], preferred_element_type=F32)
            for lt in range(lt_per_half):
                lanes = slice((hf * lt_per_half + lt) * LANE, (hf * lt_per_half + lt + 1) * LANE)
                gate = jnp.zeros((N_KEYS, LANE), F32)
                for h in range(PEER_HEADS):
                    s1 = s1T_ref[h, pl.ds(i1_base, n_i1), lanes][ii:ii + 1]
                    e1 = e1_ref[h, pl.ds(i1_base, n_i1), lanes][ii:ii + 1]
                    gate = gate + jnp.where(s1 + s2T_ref[h, :, lanes] >= thr_ref[h, :, lanes],
                                            e1 * e2_ref[h, :, lanes], 0.0)
                a = a_blk[:, lt * LANE:(lt + 1) * LANE]
                act = 0.5 * a * (1.0 + lax.erf(a * _SQRT_HALF))
                wT_ref[rows, lanes] = (gate * act).astype(BF16)
        acc_ref[:, lanes_hf] += jnp.dot(vT_ref[...], wT_ref[:, lanes_hf], preferred_element_type=F32)

    @pl.when(j == pl.num_programs(1) - 1)
    def _():
        x2 = x1_ref[...] + acc_ref[...].T
        y_ref[...] = x2 * lax.rsqrt(jnp.mean(x2 * x2, axis=-1, keepdims=True) + EPS) * gf_ref[...]


def _peer_dense(h2T, u_bf, vT_bf, s1T, s2T, thr, a0, b0, zinv, x1, lnf_g, tm):
    n = x1.shape[0]
    assert n % tm == 0
    ne = u_bf.shape[0] // _EXPERT_BLOCK
    gf = lnf_g.reshape(1, D_MODEL)
    sc = pl.BlockSpec((PEER_HEADS, N_KEYS, tm), lambda i, j: (0, 0, i))
    st = pl.BlockSpec((PEER_HEADS, 1, tm), lambda i, j: (0, 0, i))
    return pl.pallas_call(
        functools.partial(_peer_dense_kernel, tm=tm),
        grid=(n // tm, ne),
        in_specs=[pl.BlockSpec((D_MODEL, tm), lambda i, j: (0, i)),
                  pl.BlockSpec((_EXPERT_BLOCK, D_MODEL), lambda i, j: (j, 0)),
                  pl.BlockSpec((D_MODEL, _EXPERT_BLOCK), lambda i, j: (0, j)),
                  sc, sc, st, st, st, st,
                  pl.BlockSpec((tm, D_MODEL), lambda i, j: (i, 0)),
                  pl.BlockSpec((1, D_MODEL), lambda i, j: (0, 0))],
        out_specs=pl.BlockSpec((tm, D_MODEL), lambda i, j: (i, 0)),
        out_shape=jax.ShapeDtypeStruct((n, D_MODEL), F32),
        scratch_shapes=[pltpu.VMEM((PEER_HEADS, N_KEYS, tm), F32),
                        pltpu.VMEM((PEER_HEADS, N_KEYS, tm), F32),
                        pltpu.VMEM((_EXPERT_BLOCK, tm), BF16),
                        pltpu.VMEM((D_MODEL, tm), F32)],
        compiler_params=_params("parallel", "arbitrary"),
        name="peer_dense",
    )(h2T, u_bf, vT_bf, s1T, s2T, thr, a0, b0, zinv, x1, gf)


def _mix_to_output(x2d, oaT, ob, lw, tm_out, tl_sel, tm_dense):
    x1, h2T, s1T, s2T = _outproj(x2d, oaT, ob, lw["woa"], lw["wob"], lw["ln2_g"], lw["wq"],
                                 lw["k1"], lw["k2"], tm_out)
    thr, a0, b0, zinv = _peer_select(s1T, s2T, tl_sel)
    return _peer_dense(h2T, lw["u"], lw["vT"], s1T, s2T, thr, a0, b0, zinv, x1, lw["lnf_g"], tm_dense)


def _layer_prompt(x, lw):
    nb, t_len, _ = x.shape
    n = nb * t_len
    x2d = x.reshape(n, D_MODEL)
    ka, va, ki, qb, kb, vb, gb, qaT, qiT, vaT, wiT = _inproj(x2d, lw["ln1_g"], lw["wstd"], lw["wt"], 512)
    tq = 256
    nj = t_len // tq
    variants = tuple((tq * (j + 1), tq * j, tq * (j + 1)) for j in range(nj))
    oaT = _dsa(qaT, qiT, wiT, ka.reshape(nb, t_len, 128), ki.reshape(nb, t_len, 64), vaT,
               nb=nb, tq=tq, variants=variants, topk=min(TOPK_MAX, t_len // 4))
    s0 = jnp.zeros((nb, 2 * LANE, DV_B), F32)
    ob, s_fin = _retention(qb, kb, vb, gb, s0, jnp.arange(t_len, dtype=I32),
                           nb=nb, t_len=t_len, c_len=CHUNK, nsub=4)
    y = _mix_to_output(x2d, oaT, ob, lw, 512, 256, 512)
    return (y.reshape(nb, t_len, D_MODEL), ka.reshape(nb, t_len, KVH_A, DH_A),
            va.reshape(nb, t_len, KVH_A, DH_A), ki.reshape(nb, t_len, D_IDX), _state_from_packed(s_fin))


def _layer_sample(x, ck, cv, cki, cs, lw):
    nb, t_len, _ = x.shape
    past = ck.shape[1]
    n = nb * t_len
    x2d = x.reshape(n, D_MODEL)
    ka, va, ki, qb, kb, vb, gb, qaT, qiT, vaT, wiT = _inproj(x2d, lw["ln1_g"], lw["wstd"], lw["wt"], n)
    total = past + t_len
    s_pad = -(-total // LANE) * LANE
    padk = lambda a: jnp.pad(a, ((0, 0), (0, s_pad - total), (0, 0)))
    k_all = padk(jnp.concatenate([ck.reshape(nb, past, 128), ka.reshape(nb, t_len, 128)], axis=1))
    v_all = padk(jnp.concatenate([cv.reshape(nb, past, 128), va.reshape(nb, t_len, 128)], axis=1))
    ki_all = padk(jnp.concatenate([cki, ki.reshape(nb, t_len, D_IDX)], axis=1))
    vaT_all = v_all.transpose(2, 0, 1).reshape(128, nb * s_pad)
    tq = LANE
    padq = lambda a: jnp.pad(a.reshape(a.shape[0], nb, t_len),
                             ((0, 0), (0, 0), (0, tq - t_len))).reshape(a.shape[0], nb * tq)
    oaT = _dsa(padq(qaT), padq(qiT), padq(wiT), k_all, ki_all, vaT_all,
               nb=nb, tq=tq, variants=((s_pad, past, total),), topk=min(TOPK_MAX, total // 4))
    oaT = oaT.reshape(W_A, nb, tq)[:, :, :t_len].reshape(W_A, n)
    ob, s_new = _retention(qb, kb, vb, gb, _state_to_packed(cs), past + jnp.arange(t_len, dtype=I32),
                           nb=nb, t_len=t_len, c_len=t_len, nsub=1)
    y = _mix_to_output(x2d, oaT, ob, lw, n, n, n)
    return (y.reshape(nb, t_len, D_MODEL), ka.reshape(nb, t_len, KVH_A, DH_A),
            va.reshape(nb, t_len, KVH_A, DH_A), ki.reshape(nb, t_len, D_IDX), _state_from_packed(s_new))


def kernel(x_prompt, x_sample, cache_attn_k, cache_attn_v, cache_idx_k, state_retention,
           ln1_g, w_in, w_out, ln2_g, peer_w_query, peer_keys1, peer_keys2, peer_u, peer_v, ln_final_g):
    depth = w_in.shape[0]
    assert depth == 1, "the final norm is fused into the (single) layer's last kernel"
    wstd, wt = _prep_inproj_weights(w_in[0])
    lw = dict(ln1_g=ln1_g[0], wstd=wstd, wt=wt,
              woa=w_out[0, :W_A].astype(BF16), wob=w_out[0, W_A:].astype(BF16),
              ln2_g=ln2_g[0], wq=peer_w_query[0].astype(BF16),
              k1=peer_keys1[0], k2=peer_keys2[0],
              u=peer_u[0].astype(BF16), vT=peer_v[0].T.astype(BF16), lnf_g=ln_final_g)
    yp, kp, vp, kip, sp = _layer_prompt(x_prompt, lw)
    ys, ks, vs, kis, ss = _layer_sample(x_sample, cache_attn_k[0], cache_attn_v[0], cache_idx_k[0],
                                        state_retention[0], lw)
    st = lambda a: a[None]
    return (yp, ys, st(kp), st(vp), st(kip), st(sp), st(ks), st(vs), st(kis), st(ss))
```

```python
import functools

import jax
import jax.numpy as jnp
from jax import lax
from jax.experimental import pallas as pl
from jax.experimental.pallas import tpu as pltpu

F32 = jnp.float32
BF16 = jnp.bfloat16
I32 = jnp.int32

D_MODEL = 1024
CHUNK = 64
EPS = 1e-6
H_A, KVH_A, DH_A = 8, 2, 64
GRP_A = H_A // KVH_A
H_IDX, D_IDX = 8, 64
TOPK_MAX = 256
H_B, DK_B, DV_B = 4, 64, 128
ROT_BASE = 10000.0
W_A = H_A * DH_A
W_B = H_B * DV_B
N_KEYS = 128
PEER_HEADS = 8
PEER_TOPK = 16
D_PHALF = 128

LOG2E = 1.4426950408889634
INT_MIN = -(2 ** 31)
NEG_BIG = -1e30
MASKED_SCORE = -3.0e38
LANE = 128
VMEM_LIMIT_BYTES = 56 * 1024 * 1024

_NT = (((1,), (1,)), ((), ()))
_TN = (((0,), (0,)), ((), ()))


def _bdot(a, b, dims=None):
    a = a.astype(BF16)
    b = b.astype(BF16)
    if dims is None:
        return jnp.dot(a, b, preferred_element_type=F32)
    return lax.dot_general(a, b, dims, preferred_element_type=F32)


def _params(*sem):
    return pltpu.CompilerParams(dimension_semantics=sem, vmem_limit_bytes=VMEM_LIMIT_BYTES)


def _key_float(k):
    return pltpu.bitcast(k ^ ((k >> 31) & 0x7FFFFFFF), F32)


def _kth_largest_key(count_ge, k):
    kf = float(k)
    t0 = jnp.where(count_ge(0.0) >= kf, 0, INT_MIN).astype(I32)

    def bit_body(i, t):
        cand = t | lax.shift_left(jnp.int32(1), 30 - i)
        return jnp.where(count_ge(_key_float(cand)) >= kf, cand, t)

    return lax.fori_loop(0, 31, bit_body, t0)


def _inproj_kernel(x_ref, g_ref, wstd_ref, wt_ref,
                   ka_ref, va_ref, ki_ref, qb_ref, kb_ref, vb_ref, gb_ref,
                   qaT_ref, qiT_ref, vaT_ref, wiT_ref):
    x = x_ref[...]
    h = x * lax.rsqrt(jnp.mean(x * x, axis=-1, keepdims=True) + EPS) * g_ref[...]
    hb = h.astype(BF16)
    std = jnp.dot(hb, wstd_ref[...], preferred_element_type=F32)
    ka_ref[...] = std[:, 0:128]
    va_ref[...] = std[:, 128:256]
    ki_ref[...] = std[:, 256:320]
    qb_ref[...] = std[:, 384:640]
    kb_ref[...] = std[:, 640:896]
    vb_ref[...] = std[:, 896:1408]
    gb_ref[...] = std[:, 1408:1920]
    t = lax.dot_general(wt_ref[...], hb, _NT, preferred_element_type=F32)
    qaT_ref[...] = t[0:512]
    qiT_ref[...] = t[512:1024]
    vaT_ref[...] = t[1024:1152]
    wiT_ref[...] = t[1152:1160]


def _retention_perm():
    idx = []
    for half in range(2):
        for h in range(H_B):
            for d in range(DK_B // 2):
                idx.append(h * DK_B + half * (DK_B // 2) + d)
    return jnp.asarray(idx, dtype=I32)


def _prep_inproj_weights(w_in):
    o = [0]
    for s in (W_A, KVH_A * DH_A, KVH_A * DH_A, H_IDX * D_IDX, D_IDX, H_IDX,
              H_B * DK_B, H_B * DK_B, W_B, W_B):
        o.append(o[-1] + s)
    q_a, k_a, v_a, q_i, k_i, w_i, q_b, k_b, v_b, g_b = [w_in[:, o[i]:o[i + 1]] for i in range(10)]
    perm = _retention_perm()
    pad = jnp.zeros((D_MODEL, 64), w_in.dtype)
    wstd = jnp.concatenate([k_a, v_a, k_i, pad, q_b[:, perm], k_b[:, perm], v_b, g_b], axis=1)
    wt = jnp.concatenate([q_a, q_i, v_a, w_i], axis=1).T
    return wstd.astype(BF16), wt.astype(BF16)


def _inproj(x2d, ln_g, wstd, wt, tm):
    n = x2d.shape[0]
    assert n % tm == 0
    row = lambda c: pl.BlockSpec((tm, c), lambda i: (i, 0))
    col = lambda r: pl.BlockSpec((r, tm), lambda i: (0, i))
    full = lambda a: pl.BlockSpec(a.shape, lambda i: (0,) * a.ndim)
    g2 = ln_g.reshape(1, D_MODEL)
    out_shape = (
        jax.ShapeDtypeStruct((n, 128), F32), jax.ShapeDtypeStruct((n, 128), F32),
        jax.ShapeDtypeStruct((n, 64), F32),
        jax.ShapeDtypeStruct((n, 256), F32), jax.ShapeDtypeStruct((n, 256), F32),
        jax.ShapeDtypeStruct((n, 512), F32), jax.ShapeDtypeStruct((n, 512), F32),
        jax.ShapeDtypeStruct((512, n), F32), jax.ShapeDtypeStruct((512, n), F32),
        jax.ShapeDtypeStruct((128, n), F32), jax.ShapeDtypeStruct((8, n), F32),
    )
    out_specs = (row(128), row(128), row(64), row(256), row(256), row(512), row(512),
                 col(512), col(512), col(128), col(8))
    return pl.pallas_call(
        _inproj_kernel,
        grid=(n // tm,),
        in_specs=[row(D_MODEL), full(g2), full(wstd), full(wt)],
        out_specs=out_specs,
        out_shape=out_shape,
        compiler_params=_params("parallel"),
        name="inproj",
    )(x2d, g2, wstd, wt)


_KEY_ROWS = 128


def _dsa_body(qaT_ref, qiT_ref, wiT_ref, ka_ref, ki_ref, vaT_ref, oT_ref, score_ref, bias_ref,
              qpad_ref, s_ref, p_ref, m_ref, l_ref,
              *, n_keys, tq, qpos0, n_valid, topk):
    ch = _KEY_ROWS
    nch = n_keys // ch
    row_iota = lax.broadcasted_iota(I32, (ch, tq), 0)
    lane_iota = lax.broadcasted_iota(I32, (ch, tq), 1)
    q_chunk = (qpos0 + lane_iota) // CHUNK
    wi = wiT_ref[...]

    def allowed(srow):
        return ((srow // CHUNK) <= q_chunk) & (srow < n_valid)

    qi = [qiT_ref[D_IDX * h:D_IDX * (h + 1), :].astype(BF16) for h in range(H_IDX)]

    def score_chunk(c, carry):
        r0 = pl.multiple_of(c * ch, ch)
        kic = ki_ref[pl.ds(r0, ch), :].astype(BF16)
        acc = jnp.zeros((ch, tq), F32)
        for h in range(H_IDX):
            d = jnp.dot(kic, qi[h], preferred_element_type=F32)
            acc = acc + wi[h:h + 1, :] * jnp.maximum(d, 0.0)
        score_ref[pl.ds(r0, ch), :] = jnp.where(allowed(r0 + row_iota), acc, MASKED_SCORE)
        return carry

    lax.fori_loop(0, nch, score_chunk, 0)

    def count(pred):
        def body(c, acc):
            r0 = pl.multiple_of(c * ch, ch)
            x = score_ref[pl.ds(r0, ch), :]
            hit = jnp.where(pred(x, r0 + row_iota), 1.0, 0.0)
            return acc + jnp.sum(hit.reshape(ch // 8, 8, tq), axis=0)
        acc = lax.fori_loop(0, nch, body, jnp.zeros((8, tq), F32), unroll=min(nch, 4))
        return jnp.sum(acc, axis=0, keepdims=True)

    t = _kth_largest_key(lambda piv: count(lambda x, r: x >= piv), topk)
    lo = _key_float(t)
    hi = _key_float(t + 1)
    need = float(topk) - count(lambda x, r: x >= hi)

    tri = jnp.where(lax.broadcasted_iota(I32, (ch, ch), 0) > lax.broadcasted_iota(I32, (ch, ch), 1),
                    1.0, 0.0).astype(BF16)

    def bias_chunk(c, ties_before):
        r0 = pl.multiple_of(c * ch, ch)
        x = score_ref[pl.ds(r0, ch), :]
        tie = (x >= lo) & (x < hi)
        tie_f = jnp.where(tie, 1.0, 0.0)
        before = ties_before + jnp.dot(tri, tie_f.astype(BF16), preferred_element_type=F32)
        sel = ((x >= hi) | (tie & (before < need))) & allowed(r0 + row_iota)
        bias_ref[pl.ds(r0, ch), :] = jnp.where(sel, 0.0, NEG_BIG)
        return ties_before + jnp.sum(tie_f, axis=0, keepdims=True)

    lax.fori_loop(0, nch, bias_chunk, jnp.zeros((1, tq), F32))

    zeros_half = jnp.zeros((DH_A, tq), F32)
    scale = DH_A ** -0.5
    for h in range(H_A):
        qh = qaT_ref[DH_A * h:DH_A * (h + 1), :] * scale
        qpad = (jnp.concatenate([qh, zeros_half], axis=0) if h // GRP_A == 0
                else jnp.concatenate([zeros_half, qh], axis=0))
        qpad_ref[:, tq * h:tq * (h + 1)] = qpad.astype(BF16)
    m_ref[...] = jnp.full(m_ref.shape, NEG_BIG, F32)
    l_ref[...] = jnp.zeros(l_ref.shape, F32)
    oT_ref[...] = jnp.zeros(oT_ref.shape, F32)

    def att_chunk(c, carry):
        r0 = pl.multiple_of(c * ch, ch)
        kc = ka_ref[pl.ds(r0, ch), :].astype(BF16)
        bias = bias_ref[pl.ds(r0, ch), :]
        s_ref[...] = jnp.dot(kc, qpad_ref[...], preferred_element_type=F32)
        alphas = []
        for h in range(H_A):
            lanes = slice(tq * h, tq * (h + 1))
            s = s_ref[:, lanes] + bias
            m_old = m_ref[h:h + 1, :]
            m_new = jnp.maximum(m_old, jnp.max(s, axis=0, keepdims=True))
            alpha = jnp.exp(m_old - m_new)
            p = jnp.exp(s - m_new)
            l_ref[h:h + 1, :] = l_ref[h:h + 1, :] * alpha + jnp.sum(p, axis=0, keepdims=True)
            p_ref[:, lanes] = p.astype(BF16)
            m_ref[h:h + 1, :] = m_new
            alphas.append(alpha)
        for g in range(KVH_A):
            vc = vaT_ref[DH_A * g:DH_A * (g + 1), pl.ds(r0, ch)].astype(BF16)
            pv = jnp.dot(vc, p_ref[:, GRP_A * tq * g:GRP_A * tq * (g + 1)],
                         preferred_element_type=F32)
            for hh in range(GRP_A):
                h = GRP_A * g + hh
                rows = slice(DH_A * h, DH_A * (h + 1))
                oT_ref[rows, :] = oT_ref[rows, :] * alphas[h] + pv[:, tq * hh:tq * (hh + 1)]
        return carry

    lax.fori_loop(0, nch, att_chunk, 0)
    for h in range(H_A):
        rows = slice(DH_A * h, DH_A * (h + 1))
        oT_ref[rows, :] = oT_ref[rows, :] / l_ref[h:h + 1, :]


def _dsa_kernel(qaT_ref, qiT_ref, wiT_ref, ka_ref, ki_ref, vaT_ref, oT_ref, score_ref, bias_ref,
              qpad_ref, s_ref, p_ref, m_ref, l_ref,
                *, tq, variants, topk):
    j = pl.program_id(1)
    for jj, (n_keys, qpos0, n_valid) in enumerate(variants):
        @pl.when(j == jj)
        def _(n_keys=n_keys, qpos0=qpos0, n_valid=n_valid):
            _dsa_body(qaT_ref, qiT_ref, wiT_ref, ka_ref, ki_ref, vaT_ref, oT_ref, score_ref, bias_ref,
              qpad_ref, s_ref, p_ref, m_ref, l_ref,
                      n_keys=n_keys, tq=tq, qpos0=qpos0, n_valid=n_valid, topk=topk)


def _dsa(qaT, qiT, wiT, ka, ki, vaT, *, nb, tq, variants, topk):
    nj = len(variants)
    s_max = ka.shape[1]
    qspec = lambda r: pl.BlockSpec((r, tq), lambda b, j: (0, b * nj + j))
    return pl.pallas_call(
        functools.partial(_dsa_kernel, tq=tq, variants=variants, topk=topk),
        grid=(nb, nj),
        in_specs=[qspec(512), qspec(512), qspec(8),
                  pl.BlockSpec((None, s_max, 128), lambda b, j: (b, 0, 0)),
                  pl.BlockSpec((None, s_max, 64), lambda b, j: (b, 0, 0)),
                  pl.BlockSpec((128, s_max), lambda b, j: (0, b))],
        out_specs=qspec(512),
        out_shape=jax.ShapeDtypeStruct((512, nb * nj * tq), F32),
        scratch_shapes=[pltpu.VMEM((s_max, tq), F32), pltpu.VMEM((s_max, tq), F32),
                        pltpu.VMEM((KVH_A * DH_A, H_A * tq), BF16),
                        pltpu.VMEM((_KEY_ROWS, H_A * tq), F32),
                        pltpu.VMEM((_KEY_ROWS, H_A * tq), BF16),
                        pltpu.VMEM((H_A, tq), F32), pltpu.VMEM((H_A, tq), F32)],
        compiler_params=_params("parallel", "arbitrary"),
        name="dsa",
    )(qaT, qiT, wiT, ka, ki, vaT)


def _ret_kernel(q_ref, k_ref, v_ref, g_ref, cos_ref, sin_ref, dmask_ref, rdec_ref, kdec_ref,
                sdec_ref, s0_ref, o_ref, sout_ref, state_ref, *, c_len, nsub):
    ci = pl.program_id(1)

    @pl.when(ci == 0)
    def _():
        state_ref[...] = s0_ref[...]

    lane_head = (lax.broadcasted_iota(I32, (1, 2 * LANE), 1) % LANE) // (DK_B // 2)
    row_head = (lax.broadcasted_iota(I32, (2 * LANE, DV_B), 0) % LANE) // (DK_B // 2)
    for sub in range(nsub):
        rows = slice(sub * c_len, (sub + 1) * c_len)
        q = q_ref[rows, :]
        k = k_ref[rows, :]
        v = v_ref[rows, :]
        g = g_ref[rows, :]
        cos = cos_ref[rows, :]
        sin = sin_ref[rows, :]
        q1, q2 = q[:, :LANE], q[:, LANE:]
        k1, k2 = k[:, :LANE], k[:, LANE:]
        qr = jnp.concatenate([q1 * cos - q2 * sin, q1 * sin + q2 * cos], axis=1)
        kr = jnp.concatenate([k1 * cos - k2 * sin, k1 * sin + k2 * cos], axis=1) * (DK_B ** -0.5)
        state = state_ref[...]
        upd = _bdot(kr * kdec_ref[...], v, _TN)
        new_state = sdec_ref[...] * state
        vb = v.astype(BF16)
        krb = kr.astype(BF16)
        sb = state.astype(BF16)
        for h in range(H_B):
            qm = jnp.where(lane_head == h, qr, 0.0).astype(BF16)
            qk = lax.dot_general(qm, krb, _NT, preferred_element_type=F32) * dmask_ref[h]
            intra = jnp.dot(qk.astype(BF16), vb[:, DV_B * h:DV_B * (h + 1)], preferred_element_type=F32)
            inter = jnp.dot(qm, sb, preferred_element_type=F32) * rdec_ref[h]
            o = intra + inter
            mu = jnp.mean(o, axis=-1, keepdims=True)
            d = o - mu
            var = jnp.mean(d * d, axis=-1, keepdims=True)
            gh = g[:, DV_B * h:DV_B * (h + 1)]
            silu = gh * (1.0 / (1.0 + jnp.exp(-gh)))
            o_ref[rows, DV_B * h:DV_B * (h + 1)] = silu * (d * lax.rsqrt(var + EPS))
            new_state = new_state + jnp.where(row_head == h, upd[:, DV_B * h:DV_B * (h + 1)], 0.0)
        state_ref[...] = new_state

    @pl.when(ci == pl.num_programs(1) - 1)
    def _():
        sout_ref[...] = state_ref[...]


def _retention_tables(pos, c_len):
    half = DK_B // 2
    lg = jnp.log(1.0 - 2.0 ** (-5.0 - jnp.arange(H_B, dtype=F32)))
    inv = 1.0 / (ROT_BASE ** jnp.linspace(0.0, 1.0, half, dtype=F32))
    ang = pos.astype(F32)[:, None] * inv[None, :]
    cos = jnp.tile(jnp.cos(ang), (1, H_B))
    sin = jnp.tile(jnp.sin(ang), (1, H_B))
    i = jnp.arange(c_len, dtype=F32)
    diff = i[:, None] - i[None, :]
    dmask = jnp.where(diff >= 0, jnp.exp(jnp.maximum(diff, 0.0)[None] * lg[:, None, None]), 0.0)
    rdec = jnp.broadcast_to(jnp.exp((i + 1.0)[None, :] * lg[:, None])[:, :, None], (H_B, c_len, DV_B))
    lane_head = (jnp.arange(2 * LANE) % LANE) // half
    kdec = jnp.exp((c_len - 1.0 - i)[:, None] * lg[lane_head][None, :])
    sdec = jnp.broadcast_to(jnp.exp(c_len * lg[lane_head])[:, None], (2 * LANE, DV_B))
    return cos, sin, dmask.astype(F32), rdec.astype(F32), kdec.astype(F32), sdec.astype(F32)


def _state_to_packed(s):
    b = s.shape[0]
    return s.reshape(b, H_B, 2, DK_B // 2, DV_B).transpose(0, 2, 1, 3, 4).reshape(b, 2 * LANE, DV_B)


def _state_from_packed(s):
    b = s.shape[0]
    return s.reshape(b, 2, H_B, DK_B // 2, DV_B).transpose(0, 2, 1, 3, 4).reshape(b, H_B, DK_B, DV_B)


def _retention(qb, kb, vb, gb, s0_packed, pos, *, nb, t_len, c_len, nsub):
    blk = c_len * nsub
    nc = t_len // blk
    cos, sin, dmask, rdec, kdec, sdec = _retention_tables(pos, c_len)
    tok = lambda c: pl.BlockSpec((blk, c), lambda b, i: (b * nc + i, 0))
    tab = pl.BlockSpec((blk, LANE), lambda b, i: (i, 0))
    full = lambda a: pl.BlockSpec(a.shape, lambda b, i: (0,) * a.ndim)
    st = pl.BlockSpec((None, 2 * LANE, DV_B), lambda b, i: (b, 0, 0))
    return pl.pallas_call(
        functools.partial(_ret_kernel, c_len=c_len, nsub=nsub),
        grid=(nb, nc),
        in_specs=[tok(256), tok(256), tok(512), tok(512), tab, tab,
                  full(dmask), full(rdec), full(kdec), full(sdec), st],
        out_specs=(tok(512), st),
        out_shape=(jax.ShapeDtypeStruct((nb * t_len, W_B), F32),
                   jax.ShapeDtypeStruct((nb, 2 * LANE, DV_B), F32)),
        scratch_shapes=[pltpu.VMEM((2 * LANE, DV_B), F32)],
        compiler_params=_params("parallel", "arbitrary"),
        name="retention",
    )(qb, kb, vb, gb, cos, sin, dmask, rdec, kdec, sdec, s0_packed)


def _outproj_kernel(x_ref, oaT_ref, ob_ref, woa_ref, wob_ref, g2_ref, wq_ref, k1_ref, k2_ref,
                    x1_ref, h2T_ref, s1T_ref, s2T_ref):
    mix = _bdot(oaT_ref[...], woa_ref[...], _TN) + _bdot(ob_ref[...], wob_ref[...])
    x1 = x_ref[...] + mix
    x1_ref[...] = x1
    h2 = x1 * lax.rsqrt(jnp.mean(x1 * x1, axis=-1, keepdims=True) + EPS) * g2_ref[...]
    h2T_ref[...] = h2.T.astype(BF16)
    q = _bdot(h2, wq_ref[...])
    k1 = k1_ref[...]
    k2 = k2_ref[...]
    for h in range(PEER_HEADS):
        base = 2 * D_PHALF * h
        s1T_ref[h] = _bdot(k1, q[:, base:base + D_PHALF], _NT) * LOG2E
        s2T_ref[h] = _bdot(k2, q[:, base + D_PHALF:base + 2 * D_PHALF], _NT) * LOG2E


def _outproj(x2d, oaT, ob, woa, wob, ln2_g, wq, k1, k2, tm):
    n = x2d.shape[0]
    assert n % tm == 0
    g2 = ln2_g.reshape(1, D_MODEL)
    row = lambda c: pl.BlockSpec((tm, c), lambda i: (i, 0))
    col = lambda r: pl.BlockSpec((r, tm), lambda i: (0, i))
    full = lambda a: pl.BlockSpec(a.shape, lambda i: (0,) * a.ndim)
    sc = pl.BlockSpec((PEER_HEADS, N_KEYS, tm), lambda i: (0, 0, i))
    return pl.pallas_call(
        _outproj_kernel,
        grid=(n // tm,),
        in_specs=[row(D_MODEL), col(W_A), row(W_B), full(woa), full(wob), full(g2), full(wq),
                  full(k1), full(k2)],
        out_specs=(row(D_MODEL), col(D_MODEL), sc, sc),
        out_shape=(jax.ShapeDtypeStruct((n, D_MODEL), F32),
                   jax.ShapeDtypeStruct((D_MODEL, n), BF16),
                   jax.ShapeDtypeStruct((PEER_HEADS, N_KEYS, n), F32),
                   jax.ShapeDtypeStruct((PEER_HEADS, N_KEYS, n), F32)),
        compiler_params=_params("parallel"),
        name="outproj",
    )(x2d, oaT, ob, woa, wob, g2, wq, k1, k2)


def _peer_sel_kernel(s1T_ref, s2T_ref, thr_ref, shift_ref, a_ref, b_ref):
    def top16(s, dst_ref):
        for r in range(PEER_TOPK):
            m = jnp.max(s, axis=0, keepdims=True)
            dst_ref[r:r + 1, :] = m
            s = jnp.where(s == m, -jnp.inf, s)

    def head(h, carry):
        top16(s1T_ref[h], a_ref)
        top16(s2T_ref[h], b_ref)
        a = a_ref[...]
        b = b_ref[...]
        cand = [a + b[0:1]] + [a[0:8] + b[j:j + 1] for j in range(1, 8)] + [a[0:1] + b[8:16]]
        cand = jnp.concatenate(cand, axis=0)
        t = _kth_largest_key(
            lambda piv: jnp.sum(jnp.where(cand >= piv, 1.0, 0.0), axis=0, keepdims=True), PEER_TOPK)
        thr = _key_float(t)
        top = cand[0:1]
        z = jnp.sum(jnp.where(cand >= thr, jnp.exp2(cand - top), 0.0), axis=0, keepdims=True)
        thr_ref[h] = thr
        shift_ref[h] = top + jnp.log2(z)
        return carry

    lax.fori_loop(0, PEER_HEADS, head, 0)


def _peer_select(s1T, s2T, tl):
    n = s1T.shape[2]
    assert n % tl == 0
    sc = pl.BlockSpec((PEER_HEADS, N_KEYS, tl), lambda i: (0, 0, i))
    st = pl.BlockSpec((PEER_HEADS, 1, tl), lambda i: (0, 0, i))
    stat = jax.ShapeDtypeStruct((PEER_HEADS, 1, n), F32)
    return pl.pallas_call(
        _peer_sel_kernel,
        grid=(n // tl,),
        in_specs=[sc, sc],
        out_specs=(st, st),
        out_shape=(stat, stat),
        scratch_shapes=[pltpu.VMEM((PEER_TOPK, tl), F32), pltpu.VMEM((PEER_TOPK, tl), F32)],
        compiler_params=_params("parallel"),
        name="peer_select",
    )(s1T, s2T)


_I1_PER_BLOCK = 8
_EXPERT_BLOCK = _I1_PER_BLOCK * N_KEYS
_MM_ROWS = 256
_SQRT_HALF = 0.7071067811865476


def _peer_dense_kernel(h2T_ref, u_ref, vT_ref, s1T_ref, s2T_ref, thr_ref, shift_ref,
                       x1_ref, gf_ref, y_ref, aT_ref, wT_ref, acc_ref, *, tm):
    j = pl.program_id(1)
    n_i1 = _I1_PER_BLOCK
    n_lt = tm // LANE
    n_half = 2 if n_lt % 2 == 0 else 1
    lt_per_half = n_lt // n_half
    half_lanes = [slice(hf * lt_per_half * LANE, (hf + 1) * lt_per_half * LANE) for hf in range(n_half)]

    @pl.when(j == 0)
    def _():
        acc_ref[...] = jnp.zeros_like(acc_ref)

    i1_base = pl.multiple_of(j * n_i1, n_i1)

    def act_piece(hf, p):
        rows = slice(_MM_ROWS * p, _MM_ROWS * (p + 1))
        aT_ref[rows, half_lanes[hf]] = jnp.dot(u_ref[rows, :], h2T_ref[:, half_lanes[hf]],
                                               preferred_element_type=F32)

    def out_piece(hf, r):
        rows = slice(_MM_ROWS * r, _MM_ROWS * (r + 1))
        acc_ref[rows, half_lanes[hf]] += jnp.dot(vT_ref[rows, :], wT_ref[:, half_lanes[hf]],
                                                 preferred_element_type=F32)

    def gate_group(hf, ii):
        rows = slice(N_KEYS * ii, N_KEYS * (ii + 1))
        for lt in range(lt_per_half):
            lanes = slice((hf * lt_per_half + lt) * LANE, (hf * lt_per_half + lt + 1) * LANE)
            gate = jnp.zeros((N_KEYS, LANE), F32)
            for h in range(PEER_HEADS):
                s1 = s1T_ref[h, pl.ds(i1_base, n_i1), lanes][ii:ii + 1]
                sm = s1 + s2T_ref[h, :, lanes]
                gate = gate + jnp.where(sm >= thr_ref[h, :, lanes],
                                        jnp.exp2(sm - shift_ref[h, :, lanes]), 0.0)
            a = aT_ref[rows, lanes]
            act = 0.5 * a * (1.0 + lax.erf(a * _SQRT_HALF))
            wT_ref[rows, lanes] = (gate * act).astype(BF16)

    pending = [functools.partial(act_piece, hf, p) for hf in range(n_half)
               for p in range(_EXPERT_BLOCK // _MM_ROWS)]
    pending.pop(0)()
    pending.pop(0)()
    for hf in range(n_half):
        for ii in range(n_i1):
            if pending and (hf, ii) != (0, 0):
                pending.pop(0)()
            gate_group(hf, ii)
        pending.extend(functools.partial(out_piece, hf, r) for r in range(D_MODEL // _MM_ROWS))
    for piece in pending:
        piece()

    @pl.when(j == pl.num_programs(1) - 1)
    def _():
        x2 = x1_ref[...] + acc_ref[...].T
        y_ref[...] = x2 * lax.rsqrt(jnp.mean(x2 * x2, axis=-1, keepdims=True) + EPS) * gf_ref[...]


def _peer_dense(h2T, u_bf, vT_bf, s1T, s2T, thr, shift, x1, lnf_g, tm):
    n = x1.shape[0]
    assert n % tm == 0
    ne = u_bf.shape[0] // _EXPERT_BLOCK
    gf = lnf_g.reshape(1, D_MODEL)
    sc = pl.BlockSpec((PEER_HEADS, N_KEYS, tm), lambda i, j: (0, 0, i))
    st = pl.BlockSpec((PEER_HEADS, 1, tm), lambda i, j: (0, 0, i))
    return pl.pallas_call(
        functools.partial(_peer_dense_kernel, tm=tm),
        grid=(n // tm, ne),
        in_specs=[pl.BlockSpec((D_MODEL, tm), lambda i, j: (0, i)),
                  pl.BlockSpec((_EXPERT_BLOCK, D_MODEL), lambda i, j: (j, 0)),
                  pl.BlockSpec((D_MODEL, _EXPERT_BLOCK), lambda i, j: (0, j)),
                  sc, sc, st, st,
                  pl.BlockSpec((tm, D_MODEL), lambda i, j: (i, 0)),
                  pl.BlockSpec((1, D_MODEL), lambda i, j: (0, 0))],
        out_specs=pl.BlockSpec((tm, D_MODEL), lambda i, j: (i, 0)),
        out_shape=jax.ShapeDtypeStruct((n, D_MODEL), F32),
        scratch_shapes=[pltpu.VMEM((_EXPERT_BLOCK, tm), F32),
                        pltpu.VMEM((_EXPERT_BLOCK, tm), BF16),
                        pltpu.VMEM((D_MODEL, tm), F32)],
        compiler_params=_params("parallel", "arbitrary"),
        name="peer_dense",
    )(h2T, u_bf, vT_bf, s1T, s2T, thr, shift, x1, gf)


def _mix_to_output(x2d, oaT, ob, lw, tm_out, tl_sel, tm_dense):
    x1, h2T, s1T, s2T = _outproj(x2d, oaT, ob, lw["woa"], lw["wob"], lw["ln2_g"], lw["wq"],
                                 lw["k1"], lw["k2"], tm_out)
    thr, shift = _peer_select(s1T, s2T, tl_sel)
    return _peer_dense(h2T, lw["u"], lw["vT"], s1T, s2T, thr, shift, x1, lw["lnf_g"], tm_dense)


def _layer_prompt(x, lw):
    nb, t_len, _ = x.shape
    n = nb * t_len
    x2d = x.reshape(n, D_MODEL)
    ka, va, ki, qb, kb, vb, gb, qaT, qiT, vaT, wiT = _inproj(x2d, lw["ln1_g"], lw["wstd"], lw["wt"], 512)
    tq = 256
    nj = t_len // tq
    variants = tuple((tq * (j + 1), tq * j, tq * (j + 1)) for j in range(nj))
    oaT = _dsa(qaT, qiT, wiT, ka.reshape(nb, t_len, 128), ki.reshape(nb, t_len, 64), vaT,
               nb=nb, tq=tq, variants=variants, topk=min(TOPK_MAX, t_len // 4))
    s0 = jnp.zeros((nb, 2 * LANE, DV_B), F32)
    ob, s_fin = _retention(qb, kb, vb, gb, s0, jnp.arange(t_len, dtype=I32),
                           nb=nb, t_len=t_len, c_len=CHUNK, nsub=4)
    y = _mix_to_output(x2d, oaT, ob, lw, 512, 256, 512)
    return (y.reshape(nb, t_len, D_MODEL), ka.reshape(nb, t_len, KVH_A, DH_A),
            va.reshape(nb, t_len, KVH_A, DH_A), ki.reshape(nb, t_len, D_IDX), _state_from_packed(s_fin))


def _layer_sample(x, ck, cv, cki, cs, lw):
    nb, t_len, _ = x.shape
    past = ck.shape[1]
    n = nb * t_len
    x2d = x.reshape(n, D_MODEL)
    ka, va, ki, qb, kb, vb, gb, qaT, qiT, vaT, wiT = _inproj(x2d, lw["ln1_g"], lw["wstd"], lw["wt"], n)
    total = past + t_len
    s_pad = -(-total // LANE) * LANE
    padk = lambda a: jnp.pad(a, ((0, 0), (0, s_pad - total), (0, 0)))
    k_all = padk(jnp.concatenate([ck.reshape(nb, past, 128), ka.reshape(nb, t_len, 128)], axis=1))
    v_all = padk(jnp.concatenate([cv.reshape(nb, past, 128), va.reshape(nb, t_len, 128)], axis=1))
    ki_all = padk(jnp.concatenate([cki, ki.reshape(nb, t_len, D_IDX)], axis=1))
    vaT_all = v_all.transpose(2, 0, 1).reshape(128, nb * s_pad)
    tq = LANE
    padq = lambda a: jnp.pad(a.reshape(a.shape[0], nb, t_len),
                             ((0, 0), (0, 0), (0, tq - t_len))).reshape(a.shape[0], nb * tq)
    oaT = _dsa(padq(qaT), padq(qiT), padq(wiT), k_all, ki_all, vaT_all,
               nb=nb, tq=tq, variants=((s_pad, past, total),), topk=min(TOPK_MAX, total // 4))
    oaT = oaT.reshape(W_A, nb, tq)[:, :, :t_len].reshape(W_A, n)
    ob, s_new = _retention(qb, kb, vb, gb, _state_to_packed(cs), past + jnp.arange(t_len, dtype=I32),
                           nb=nb, t_len=t_len, c_len=t_len, nsub=1)
    y = _mix_to_output(x2d, oaT, ob, lw, n, n, n)
    return (y.reshape(nb, t_len, D_MODEL), ka.reshape(nb, t_len, KVH_A, DH_A),
            va.reshape(nb, t_len, KVH_A, DH_A), ki.reshape(nb, t_len, D_IDX), _state_from_packed(s_new))


def kernel(x_prompt, x_sample, cache_attn_k, cache_attn_v, cache_idx_k, state_retention,
           ln1_g, w_in, w_out, ln2_g, peer_w_query, peer_keys1, peer_keys2, peer_u, peer_v, ln_final_g):
    depth = w_in.shape[0]
    assert depth == 1, "the final norm is fused into the (single) layer's last kernel"
    wstd, wt = _prep_inproj_weights(w_in[0])
    lw = dict(ln1_g=ln1_g[0], wstd=wstd, wt=wt,
              woa=w_out[0, :W_A].astype(BF16), wob=w_out[0, W_A:].astype(BF16),
              ln2_g=ln2_g[0], wq=peer_w_query[0].astype(BF16),
              k1=peer_keys1[0], k2=peer_keys2[0],
              u=peer_u[0].astype(BF16), vT=peer_v[0].T.astype(BF16), lnf_g=ln_final_g)
    yp, kp, vp, kip, sp = _layer_prompt(x_prompt, lw)
    ys, ks, vs, kis, ss = _layer_sample(x_sample, cache_attn_k[0], cache_attn_v[0], cache_idx_k[0],
                                        state_retention[0], lw)
    st = lambda a: a[None]
    return (yp, ys, st(kp), st(vp), st(kip), st(sp), st(ks), st(vs), st(kis), st(ss))
```

```python
import functools

import jax
import jax.numpy as jnp
from jax import lax
from jax.experimental import pallas as pl
from jax.experimental.pallas import tpu as pltpu

F32 = jnp.float32
BF16 = jnp.bfloat16
I32 = jnp.int32

D_MODEL = 1024
CHUNK = 64
EPS = 1e-6
H_A, KVH_A, DH_A = 8, 2, 64
GRP_A = H_A // KVH_A
H_IDX, D_IDX = 8, 64
TOPK_MAX = 256
H_B, DK_B, DV_B = 4, 64, 128
ROT_BASE = 10000.0
W_A = H_A * DH_A
W_B = H_B * DV_B
N_KEYS = 128
PEER_HEADS = 8
PEER_TOPK = 16
D_PHALF = 128

LOG2E = 1.4426950408889634
INT_MIN = -(2 ** 31)
NEG_BIG = -1e30
MASKED_SCORE = -3.0e38
LANE = 128
VMEM_LIMIT_BYTES = 56 * 1024 * 1024

_NT = (((1,), (1,)), ((), ()))
_TN = (((0,), (0,)), ((), ()))


def _bdot(a, b, dims=None):
    a = a.astype(BF16)
    b = b.astype(BF16)
    if dims is None:
        return jnp.dot(a, b, preferred_element_type=F32)
    return lax.dot_general(a, b, dims, preferred_element_type=F32)


def _params(*sem):
    return pltpu.CompilerParams(dimension_semantics=sem, vmem_limit_bytes=VMEM_LIMIT_BYTES)


def _key_float(k):
    return pltpu.bitcast(k ^ ((k >> 31) & 0x7FFFFFFF), F32)


def _kth_largest_key(count_ge, k):
    kf = float(k)
    t0 = jnp.where(count_ge(0.0) >= kf, 0, INT_MIN).astype(I32)

    def bit_body(i, t):
        cand = t | lax.shift_left(jnp.int32(1), 30 - i)
        return jnp.where(count_ge(_key_float(cand)) >= kf, cand, t)

    return lax.fori_loop(0, 31, bit_body, t0)


def _inproj_kernel(x_ref, g_ref, wstd_ref, wt_ref,
                   ka_ref, va_ref, ki_ref, qb_ref, kb_ref, vb_ref, gb_ref,
                   qaT_ref, qiT_ref, vaT_ref, wiT_ref):
    x = x_ref[...]
    h = x * lax.rsqrt(jnp.mean(x * x, axis=-1, keepdims=True) + EPS) * g_ref[...]
    hb = h.astype(BF16)
    std = jnp.dot(hb, wstd_ref[...], preferred_element_type=F32)
    ka_ref[...] = std[:, 0:128]
    va_ref[...] = std[:, 128:256]
    ki_ref[...] = std[:, 256:320]
    qb_ref[...] = std[:, 384:640]
    kb_ref[...] = std[:, 640:896]
    vb_ref[...] = std[:, 896:1408]
    gb_ref[...] = std[:, 1408:1920]
    t = lax.dot_general(wt_ref[...], hb, _NT, preferred_element_type=F32)
    qaT_ref[...] = t[0:512]
    qiT_ref[...] = t[512:1024]
    vaT_ref[...] = t[1024:1152]
    wiT_ref[...] = t[1152:1160]


def _retention_perm():
    idx = []
    for half in range(2):
        for h in range(H_B):
            for d in range(DK_B // 2):
                idx.append(h * DK_B + half * (DK_B // 2) + d)
    return jnp.asarray(idx, dtype=I32)


def _prep_inproj_weights(w_in):
    o = [0]
    for s in (W_A, KVH_A * DH_A, KVH_A * DH_A, H_IDX * D_IDX, D_IDX, H_IDX,
              H_B * DK_B, H_B * DK_B, W_B, W_B):
        o.append(o[-1] + s)
    q_a, k_a, v_a, q_i, k_i, w_i, q_b, k_b, v_b, g_b = [w_in[:, o[i]:o[i + 1]] for i in range(10)]
    perm = _retention_perm()
    pad = jnp.zeros((D_MODEL, 64), w_in.dtype)
    wstd = jnp.concatenate([k_a, v_a, k_i, pad, q_b[:, perm], k_b[:, perm], v_b, g_b], axis=1)
    wt = jnp.concatenate([q_a, q_i, v_a, w_i], axis=1).T
    return wstd.astype(BF16), wt.astype(BF16)


def _inproj(x2d, ln_g, wstd, wt, tm):
    n = x2d.shape[0]
    assert n % tm == 0
    row = lambda c: pl.BlockSpec((tm, c), lambda i: (i, 0))
    col = lambda r: pl.BlockSpec((r, tm), lambda i: (0, i))
    full = lambda a: pl.BlockSpec(a.shape, lambda i: (0,) * a.ndim)
    g2 = ln_g.reshape(1, D_MODEL)
    out_shape = (
        jax.ShapeDtypeStruct((n, 128), F32), jax.ShapeDtypeStruct((n, 128), F32),
        jax.ShapeDtypeStruct((n, 64), F32),
        jax.ShapeDtypeStruct((n, 256), F32), jax.ShapeDtypeStruct((n, 256), F32),
        jax.ShapeDtypeStruct((n, 512), F32), jax.ShapeDtypeStruct((n, 512), F32),
        jax.ShapeDtypeStruct((512, n), F32), jax.ShapeDtypeStruct((512, n), F32),
        jax.ShapeDtypeStruct((128, n), F32), jax.ShapeDtypeStruct((8, n), F32),
    )
    out_specs = (row(128), row(128), row(64), row(256), row(256), row(512), row(512),
                 col(512), col(512), col(128), col(8))
    return pl.pallas_call(
        _inproj_kernel,
        grid=(n // tm,),
        in_specs=[row(D_MODEL), full(g2), full(wstd), full(wt)],
        out_specs=out_specs,
        out_shape=out_shape,
        compiler_params=_params("parallel"),
        name="inproj",
    )(x2d, g2, wstd, wt)


_KEY_ROWS = 128


def _dsa_body(qaT_ref, qiT_ref, wiT_ref, ka_ref, ki_ref, vaT_ref, oT_ref, score_ref, bias_ref,
              qpad_ref, s_ref, p_ref, m_ref, l_ref,
              *, n_keys, tq, qpos0, n_valid, topk):
    ch = _KEY_ROWS
    nch = n_keys // ch
    row_iota = lax.broadcasted_iota(I32, (ch, tq), 0)
    lane_iota = lax.broadcasted_iota(I32, (ch, tq), 1)
    q_chunk = (qpos0 + lane_iota) // CHUNK
    wi = wiT_ref[...]

    def allowed(srow):
        return ((srow // CHUNK) <= q_chunk) & (srow < n_valid)

    qi = [qiT_ref[D_IDX * h:D_IDX * (h + 1), :].astype(BF16) for h in range(H_IDX)]

    def score_chunk(c, carry):
        r0 = pl.multiple_of(c * ch, ch)
        kic = ki_ref[pl.ds(r0, ch), :].astype(BF16)
        acc = jnp.zeros((ch, tq), F32)
        for h in range(H_IDX):
            d = jnp.dot(kic, qi[h], preferred_element_type=F32)
            acc = acc + wi[h:h + 1, :] * jnp.maximum(d, 0.0)
        score_ref[pl.ds(r0, ch), :] = jnp.where(allowed(r0 + row_iota), acc, MASKED_SCORE)
        return carry

    lax.fori_loop(0, nch, score_chunk, 0)

    def count(pred):
        def body(c, acc):
            r0 = pl.multiple_of(c * ch, ch)
            x = score_ref[pl.ds(r0, ch), :]
            hit = jnp.where(pred(x, r0 + row_iota), 1.0, 0.0)
            return acc + jnp.sum(hit.reshape(ch // 8, 8, tq), axis=0)
        acc = lax.fori_loop(0, nch, body, jnp.zeros((8, tq), F32), unroll=min(nch, 4))
        return jnp.sum(acc, axis=0, keepdims=True)

    t = _kth_largest_key(lambda piv: count(lambda x, r: x >= piv), topk)
    lo = _key_float(t)
    hi = _key_float(t + 1)
    need = float(topk) - count(lambda x, r: x >= hi)

    tri = jnp.where(lax.broadcasted_iota(I32, (ch, ch), 0) > lax.broadcasted_iota(I32, (ch, ch), 1),
                    1.0, 0.0).astype(BF16)

    def bias_chunk(c, ties_before):
        r0 = pl.multiple_of(c * ch, ch)
        x = score_ref[pl.ds(r0, ch), :]
        tie = (x >= lo) & (x < hi)
        tie_f = jnp.where(tie, 1.0, 0.0)
        before = ties_before + jnp.dot(tri, tie_f.astype(BF16), preferred_element_type=F32)
        sel = ((x >= hi) | (tie & (before < need))) & allowed(r0 + row_iota)
        bias_ref[pl.ds(r0, ch), :] = jnp.where(sel, 0.0, NEG_BIG)
        return ties_before + jnp.sum(tie_f, axis=0, keepdims=True)

    lax.fori_loop(0, nch, bias_chunk, jnp.zeros((1, tq), F32))

    zeros_half = jnp.zeros((DH_A, tq), F32)
    scale = DH_A ** -0.5
    for h in range(H_A):
        qh = qaT_ref[DH_A * h:DH_A * (h + 1), :] * scale
        qpad = (jnp.concatenate([qh, zeros_half], axis=0) if h // GRP_A == 0
                else jnp.concatenate([zeros_half, qh], axis=0))
        qpad_ref[:, tq * h:tq * (h + 1)] = qpad.astype(BF16)
    m_ref[...] = jnp.full(m_ref.shape, NEG_BIG, F32)
    l_ref[...] = jnp.zeros(l_ref.shape, F32)
    oT_ref[...] = jnp.zeros(oT_ref.shape, F32)

    def att_chunk(c, carry):
        r0 = pl.multiple_of(c * ch, ch)
        kc = ka_ref[pl.ds(r0, ch), :].astype(BF16)
        bias = bias_ref[pl.ds(r0, ch), :]
        s_ref[...] = jnp.dot(kc, qpad_ref[...], preferred_element_type=F32)
        alphas = []
        for h in range(H_A):
            lanes = slice(tq * h, tq * (h + 1))
            s = s_ref[:, lanes] + bias
            m_old = m_ref[h:h + 1, :]
            m_new = jnp.maximum(m_old, jnp.max(s, axis=0, keepdims=True))
            alpha = jnp.exp(m_old - m_new)
            p = jnp.exp(s - m_new)
            l_ref[h:h + 1, :] = l_ref[h:h + 1, :] * alpha + jnp.sum(p, axis=0, keepdims=True)
            p_ref[:, lanes] = p.astype(BF16)
            m_ref[h:h + 1, :] = m_new
            alphas.append(alpha)
        for g in range(KVH_A):
            vc = vaT_ref[DH_A * g:DH_A * (g + 1), pl.ds(r0, ch)].astype(BF16)
            pv = jnp.dot(vc, p_ref[:, GRP_A * tq * g:GRP_A * tq * (g + 1)],
                         preferred_element_type=F32)
            for hh in range(GRP_A):
                h = GRP_A * g + hh
                rows = slice(DH_A * h, DH_A * (h + 1))
                oT_ref[rows, :] = oT_ref[rows, :] * alphas[h] + pv[:, tq * hh:tq * (hh + 1)]
        return carry

    lax.fori_loop(0, nch, att_chunk, 0)
    for h in range(H_A):
        rows = slice(DH_A * h, DH_A * (h + 1))
        oT_ref[rows, :] = oT_ref[rows, :] / l_ref[h:h + 1, :]


def _dsa_kernel(qaT_ref, qiT_ref, wiT_ref, ka_ref, ki_ref, vaT_ref, oT_ref, score_ref, bias_ref,
              qpad_ref, s_ref, p_ref, m_ref, l_ref,
                *, tq, variants, topk):
    j = pl.program_id(1)
    for jj, (n_keys, qpos0, n_valid) in enumerate(variants):
        @pl.when(j == jj)
        def _(n_keys=n_keys, qpos0=qpos0, n_valid=n_valid):
            _dsa_body(qaT_ref, qiT_ref, wiT_ref, ka_ref, ki_ref, vaT_ref, oT_ref, score_ref, bias_ref,
              qpad_ref, s_ref, p_ref, m_ref, l_ref,
                      n_keys=n_keys, tq=tq, qpos0=qpos0, n_valid=n_valid, topk=topk)


def _dsa(qaT, qiT, wiT, ka, ki, vaT, *, nb, tq, variants, topk):
    nj = len(variants)
    s_max = ka.shape[1]
    qspec = lambda r: pl.BlockSpec((r, tq), lambda b, j: (0, b * nj + j))
    return pl.pallas_call(
        functools.partial(_dsa_kernel, tq=tq, variants=variants, topk=topk),
        grid=(nb, nj),
        in_specs=[qspec(512), qspec(512), qspec(8),
                  pl.BlockSpec((None, s_max, 128), lambda b, j: (b, 0, 0)),
                  pl.BlockSpec((None, s_max, 64), lambda b, j: (b, 0, 0)),
                  pl.BlockSpec((128, s_max), lambda b, j: (0, b))],
        out_specs=qspec(512),
        out_shape=jax.ShapeDtypeStruct((512, nb * nj * tq), F32),
        scratch_shapes=[pltpu.VMEM((s_max, tq), F32), pltpu.VMEM((s_max, tq), F32),
                        pltpu.VMEM((KVH_A * DH_A, H_A * tq), BF16),
                        pltpu.VMEM((_KEY_ROWS, H_A * tq), F32),
                        pltpu.VMEM((_KEY_ROWS, H_A * tq), BF16),
                        pltpu.VMEM((H_A, tq), F32), pltpu.VMEM((H_A, tq), F32)],
        compiler_params=_params("parallel", "arbitrary"),
        name="dsa",
    )(qaT, qiT, wiT, ka, ki, vaT)


def _dsa_few_kernel(qa_ref, qi_ref, wi_ref, ka_ref, ki_ref, va_ref, o_ref, *, nq, n_keys, qpos0, n_valid, topk):
    kib = ki_ref[...].astype(BF16)
    dots = lax.dot_general(qi_ref[...].astype(BF16), kib, _NT, preferred_element_type=F32)
    wi = wi_ref[...]
    score = jnp.zeros((nq, n_keys), F32)
    for h in range(H_IDX):
        score = score + wi[:, h:h + 1] * jnp.maximum(dots[nq * h:nq * (h + 1), :], 0.0)
    key_pos = lax.broadcasted_iota(I32, (nq, n_keys), 1)
    q_pos = qpos0 + lax.broadcasted_iota(I32, (nq, n_keys), 0)
    allowed = ((key_pos // CHUNK) <= (q_pos // CHUNK)) & (key_pos < n_valid)
    x = jnp.where(allowed, score, MASKED_SCORE)

    def count(pred):
        return jnp.sum(jnp.where(pred, 1.0, 0.0), axis=1, keepdims=True)

    t = _kth_largest_key(lambda piv: count(x >= piv), topk)
    lo = _key_float(t)
    hi = _key_float(t + 1)
    need = float(topk) - count(x >= hi)
    tie = (x >= lo) & (x < hi)
    tie_f = jnp.where(tie, 1.0, 0.0)
    tri = jnp.where(lax.broadcasted_iota(I32, (LANE, LANE), 0) < lax.broadcasted_iota(I32, (LANE, LANE), 1),
                    1.0, 0.0).astype(BF16)
    running = jnp.zeros((nq, 1), F32)
    before = []
    for c in range(n_keys // LANE):
        blk = tie_f[:, LANE * c:LANE * (c + 1)]
        before.append(running + jnp.dot(blk.astype(BF16), tri, preferred_element_type=F32))
        running = running + jnp.sum(blk, axis=1, keepdims=True)
    before = jnp.concatenate(before, axis=1)
    sel = ((x >= hi) | (tie & (before < need))) & allowed
    bias = jnp.where(sel, 0.0, NEG_BIG)

    kab = ka_ref[...].astype(BF16)
    s = lax.dot_general(qa_ref[...].astype(BF16), kab, _NT, preferred_element_type=F32)
    s = s + jnp.concatenate([bias] * H_A, axis=0)
    m = jnp.max(s, axis=1, keepdims=True)
    p = jnp.exp(s - m)
    l = jnp.sum(p, axis=1, keepdims=True)
    pv = jnp.dot(p.astype(BF16), va_ref[...].astype(BF16), preferred_element_type=F32) / l
    for h in range(H_A):
        g = h // GRP_A
        o_ref[:, DH_A * h:DH_A * (h + 1)] = pv[nq * h:nq * (h + 1), DH_A * g:DH_A * (g + 1)]


def _dsa_few(qa_rows, qi_rows, wi, ka, ki, va, *, nq, qpos0, n_valid, topk):
    nb, n_keys = ka.shape[0], ka.shape[1]
    blk = lambda r, c: pl.BlockSpec((None, r, c), lambda b: (b, 0, 0))
    return pl.pallas_call(
        functools.partial(_dsa_few_kernel, nq=nq, n_keys=n_keys, qpos0=qpos0, n_valid=n_valid, topk=topk),
        grid=(nb,),
        in_specs=[blk(H_A * nq, 128), blk(H_IDX * nq, 64), blk(nq, H_IDX),
                  blk(n_keys, 128), blk(n_keys, 64), blk(n_keys, 128)],
        out_specs=pl.BlockSpec((nq, W_A), lambda b: (b, 0)),
        out_shape=jax.ShapeDtypeStruct((nb * nq, W_A), F32),
        compiler_params=_params("parallel"),
        name="dsa_few",
    )(qa_rows, qi_rows, wi, ka, ki, va)


def _ret_kernel(q_ref, k_ref, v_ref, g_ref, cos_ref, sin_ref, dmask_ref, rdec_ref, kdec_ref,
                sdec_ref, s0_ref, o_ref, sout_ref, state_ref, *, c_len, nsub):
    ci = pl.program_id(1)

    @pl.when(ci == 0)
    def _():
        state_ref[...] = s0_ref[...]

    lane_head = (lax.broadcasted_iota(I32, (1, 2 * LANE), 1) % LANE) // (DK_B // 2)
    row_head = (lax.broadcasted_iota(I32, (2 * LANE, DV_B), 0) % LANE) // (DK_B // 2)
    for sub in range(nsub):
        rows = slice(sub * c_len, (sub + 1) * c_len)
        q = q_ref[rows, :]
        k = k_ref[rows, :]
        v = v_ref[rows, :]
        g = g_ref[rows, :]
        cos = cos_ref[rows, :]
        sin = sin_ref[rows, :]
        q1, q2 = q[:, :LANE], q[:, LANE:]
        k1, k2 = k[:, :LANE], k[:, LANE:]
        qr = jnp.concatenate([q1 * cos - q2 * sin, q1 * sin + q2 * cos], axis=1)
        kr = jnp.concatenate([k1 * cos - k2 * sin, k1 * sin + k2 * cos], axis=1) * (DK_B ** -0.5)
        state = state_ref[...]
        upd = _bdot(kr * kdec_ref[...], v, _TN)
        new_state = sdec_ref[...] * state
        vb = v.astype(BF16)
        krb = kr.astype(BF16)
        sb = state.astype(BF16)
        for h in range(H_B):
            qm = jnp.where(lane_head == h, qr, 0.0).astype(BF16)
            qk = lax.dot_general(qm, krb, _NT, preferred_element_type=F32) * dmask_ref[h]
            intra = jnp.dot(qk.astype(BF16), vb[:, DV_B * h:DV_B * (h + 1)], preferred_element_type=F32)
            inter = jnp.dot(qm, sb, preferred_element_type=F32) * rdec_ref[h]
            o = intra + inter
            mu = jnp.mean(o, axis=-1, keepdims=True)
            d = o - mu
            var = jnp.mean(d * d, axis=-1, keepdims=True)
            gh = g[:, DV_B * h:DV_B * (h + 1)]
            silu = gh * (1.0 / (1.0 + jnp.exp(-gh)))
            o_ref[rows, DV_B * h:DV_B * (h + 1)] = silu * (d * lax.rsqrt(var + EPS))
            new_state = new_state + jnp.where(row_head == h, upd[:, DV_B * h:DV_B * (h + 1)], 0.0)
        state_ref[...] = new_state

    @pl.when(ci == pl.num_programs(1) - 1)
    def _():
        sout_ref[...] = state_ref[...]


def _retention_tables(pos, c_len):
    half = DK_B // 2
    lg = jnp.log(1.0 - 2.0 ** (-5.0 - jnp.arange(H_B, dtype=F32)))
    inv = 1.0 / (ROT_BASE ** jnp.linspace(0.0, 1.0, half, dtype=F32))
    ang = pos.astype(F32)[:, None] * inv[None, :]
    cos = jnp.tile(jnp.cos(ang), (1, H_B))
    sin = jnp.tile(jnp.sin(ang), (1, H_B))
    i = jnp.arange(c_len, dtype=F32)
    diff = i[:, None] - i[None, :]
    dmask = jnp.where(diff >= 0, jnp.exp(jnp.maximum(diff, 0.0)[None] * lg[:, None, None]), 0.0)
    rdec = jnp.broadcast_to(jnp.exp((i + 1.0)[None, :] * lg[:, None])[:, :, None], (H_B, c_len, DV_B))
    lane_head = (jnp.arange(2 * LANE) % LANE) // half
    kdec = jnp.exp((c_len - 1.0 - i)[:, None] * lg[lane_head][None, :])
    sdec = jnp.broadcast_to(jnp.exp(c_len * lg[lane_head])[:, None], (2 * LANE, DV_B))
    return cos, sin, dmask.astype(F32), rdec.astype(F32), kdec.astype(F32), sdec.astype(F32)


def _state_to_packed(s):
    b = s.shape[0]
    return s.reshape(b, H_B, 2, DK_B // 2, DV_B).transpose(0, 2, 1, 3, 4).reshape(b, 2 * LANE, DV_B)


def _state_from_packed(s):
    b = s.shape[0]
    return s.reshape(b, 2, H_B, DK_B // 2, DV_B).transpose(0, 2, 1, 3, 4).reshape(b, H_B, DK_B, DV_B)


def _retention(qb, kb, vb, gb, s0_packed, pos, *, nb, t_len, c_len, nsub):
    blk = c_len * nsub
    nc = t_len // blk
    cos, sin, dmask, rdec, kdec, sdec = _retention_tables(pos, c_len)
    tok = lambda c: pl.BlockSpec((blk, c), lambda b, i: (b * nc + i, 0))
    tab = pl.BlockSpec((blk, LANE), lambda b, i: (i, 0))
    full = lambda a: pl.BlockSpec(a.shape, lambda b, i: (0,) * a.ndim)
    st = pl.BlockSpec((None, 2 * LANE, DV_B), lambda b, i: (b, 0, 0))
    return pl.pallas_call(
        functools.partial(_ret_kernel, c_len=c_len, nsub=nsub),
        grid=(nb, nc),
        in_specs=[tok(256), tok(256), tok(512), tok(512), tab, tab,
                  full(dmask), full(rdec), full(kdec), full(sdec), st],
        out_specs=(tok(512), st),
        out_shape=(jax.ShapeDtypeStruct((nb * t_len, W_B), F32),
                   jax.ShapeDtypeStruct((nb, 2 * LANE, DV_B), F32)),
        scratch_shapes=[pltpu.VMEM((2 * LANE, DV_B), F32)],
        compiler_params=_params("parallel", "arbitrary"),
        name="retention",
    )(qb, kb, vb, gb, cos, sin, dmask, rdec, kdec, sdec, s0_packed)


def _outproj_kernel(x_ref, oaT_ref, ob_ref, woa_ref, wob_ref, g2_ref, wq_ref, k1_ref, k2_ref,
                    x1_ref, h2T_ref, s1T_ref, s2T_ref):
    mix = _bdot(oaT_ref[...], woa_ref[...], _TN) + _bdot(ob_ref[...], wob_ref[...])
    x1 = x_ref[...] + mix
    x1_ref[...] = x1
    h2 = x1 * lax.rsqrt(jnp.mean(x1 * x1, axis=-1, keepdims=True) + EPS) * g2_ref[...]
    h2T_ref[...] = h2.T.astype(BF16)
    q = _bdot(h2, wq_ref[...])
    k1 = k1_ref[...]
    k2 = k2_ref[...]
    for h in range(PEER_HEADS):
        base = 2 * D_PHALF * h
        s1T_ref[h] = _bdot(k1, q[:, base:base + D_PHALF], _NT) * LOG2E
        s2T_ref[h] = _bdot(k2, q[:, base + D_PHALF:base + 2 * D_PHALF], _NT) * LOG2E


def _outproj(x2d, oaT, ob, woa, wob, ln2_g, wq, k1, k2, tm):
    n = x2d.shape[0]
    assert n % tm == 0
    g2 = ln2_g.reshape(1, D_MODEL)
    row = lambda c: pl.BlockSpec((tm, c), lambda i: (i, 0))
    col = lambda r: pl.BlockSpec((r, tm), lambda i: (0, i))
    full = lambda a: pl.BlockSpec(a.shape, lambda i: (0,) * a.ndim)
    sc = pl.BlockSpec((PEER_HEADS, N_KEYS, tm), lambda i: (0, 0, i))
    return pl.pallas_call(
        _outproj_kernel,
        grid=(n // tm,),
        in_specs=[row(D_MODEL), col(W_A), row(W_B), full(woa), full(wob), full(g2), full(wq),
                  full(k1), full(k2)],
        out_specs=(row(D_MODEL), col(D_MODEL), sc, sc),
        out_shape=(jax.ShapeDtypeStruct((n, D_MODEL), F32),
                   jax.ShapeDtypeStruct((D_MODEL, n), BF16),
                   jax.ShapeDtypeStruct((PEER_HEADS, N_KEYS, n), F32),
                   jax.ShapeDtypeStruct((PEER_HEADS, N_KEYS, n), F32)),
        compiler_params=_params("parallel"),
        name="outproj",
    )(x2d, oaT, ob, woa, wob, g2, wq, k1, k2)


def _peer_sel_kernel(s1T_ref, s2T_ref, thr_ref, shift_ref, a_ref, b_ref, cand_ref):
    def top16(s, dst_ref):
        for r in range(PEER_TOPK):
            m = jnp.max(s, axis=0, keepdims=True)
            dst_ref[r:r + 1, :] = m
            s = jnp.where(s == m, -jnp.inf, s)

    def candidates(h, carry):
        top16(s1T_ref[h], a_ref)
        top16(s2T_ref[h], b_ref)
        a = a_ref[...]
        b = b_ref[...]
        cand = [a + b[0:1]] + [a[0:8] + b[j:j + 1] for j in range(1, 8)] + [a[0:1] + b[8:16]]
        cand_ref[h] = jnp.concatenate(cand, axis=0)
        return carry

    lax.fori_loop(0, PEER_HEADS, candidates, 0)

    def count_ge(piv):
        rows = []
        for h in range(PEER_HEADS):
            p = piv if jnp.ndim(piv) == 0 else piv[h:h + 1, :]
            rows.append(jnp.sum(jnp.where(cand_ref[h] >= p, 1.0, 0.0), axis=0, keepdims=True))
        return jnp.concatenate(rows, axis=0)

    thr_all = _key_float(_kth_largest_key(count_ge, PEER_TOPK))
    for h in range(PEER_HEADS):
        cand = cand_ref[h]
        thr = thr_all[h:h + 1, :]
        top = cand[0:1]
        z = jnp.sum(jnp.where(cand >= thr, jnp.exp2(cand - top), 0.0), axis=0, keepdims=True)
        thr_ref[h] = thr
        shift_ref[h] = top + jnp.log2(z)


def _peer_select(s1T, s2T, tl):
    n = s1T.shape[2]
    assert n % tl == 0
    sc = pl.BlockSpec((PEER_HEADS, N_KEYS, tl), lambda i: (0, 0, i))
    st = pl.BlockSpec((PEER_HEADS, 1, tl), lambda i: (0, 0, i))
    stat = jax.ShapeDtypeStruct((PEER_HEADS, 1, n), F32)
    return pl.pallas_call(
        _peer_sel_kernel,
        grid=(n // tl,),
        in_specs=[sc, sc],
        out_specs=(st, st),
        out_shape=(stat, stat),
        scratch_shapes=[pltpu.VMEM((PEER_TOPK, tl), F32), pltpu.VMEM((PEER_TOPK, tl), F32),
                        pltpu.VMEM((PEER_HEADS, 80, tl), F32)],
        compiler_params=_params("parallel"),
        name="peer_select",
    )(s1T, s2T)


_I1_PER_BLOCK = 8
_EXPERT_BLOCK = _I1_PER_BLOCK * N_KEYS
_MM_ROWS = 256
_SQRT_HALF = 0.7071067811865476


def _peer_dense_kernel(h2T_ref, u_ref, vT_ref, s1T_ref, s2T_ref, thr_ref, shift_ref,
                       x1_ref, gf_ref, y_ref, aT_ref, wT_ref, acc_ref, *, tm):
    j = pl.program_id(1)
    n_i1 = _I1_PER_BLOCK
    n_lt = tm // LANE
    n_half = 2 if n_lt % 2 == 0 else 1
    lt_per_half = n_lt // n_half
    half_lanes = [slice(hf * lt_per_half * LANE, (hf + 1) * lt_per_half * LANE) for hf in range(n_half)]

    @pl.when(j == 0)
    def _():
        acc_ref[...] = jnp.zeros_like(acc_ref)

    i1_base = pl.multiple_of(j * n_i1, n_i1)

    def act_piece(hf, p):
        rows = slice(_MM_ROWS * p, _MM_ROWS * (p + 1))
        aT_ref[rows, half_lanes[hf]] = jnp.dot(u_ref[rows, :], h2T_ref[:, half_lanes[hf]],
                                               preferred_element_type=F32)

    def out_piece(hf, r):
        rows = slice(_MM_ROWS * r, _MM_ROWS * (r + 1))
        acc_ref[rows, half_lanes[hf]] += jnp.dot(vT_ref[rows, :], wT_ref[:, half_lanes[hf]],
                                                 preferred_element_type=F32)

    def gate_group(hf, ii):
        rows = slice(N_KEYS * ii, N_KEYS * (ii + 1))
        for lt in range(lt_per_half):
            lanes = slice((hf * lt_per_half + lt) * LANE, (hf * lt_per_half + lt + 1) * LANE)
            gate = jnp.zeros((N_KEYS, LANE), F32)
            for h in range(PEER_HEADS):
                s1 = s1T_ref[h, pl.ds(i1_base, n_i1), lanes][ii:ii + 1]
                sm = s1 + s2T_ref[h, :, lanes]
                gate = gate + jnp.where(sm >= thr_ref[h, :, lanes],
                                        jnp.exp2(sm - shift_ref[h, :, lanes]), 0.0)
            a = aT_ref[rows, lanes]
            act = 0.5 * a * (1.0 + lax.erf(a * _SQRT_HALF))
            wT_ref[rows, lanes] = (gate * act).astype(BF16)

    pending = [functools.partial(act_piece, hf, p) for hf in range(n_half)
               for p in range(_EXPERT_BLOCK // _MM_ROWS)]
    pending.pop(0)()
    pending.pop(0)()
    for hf in range(n_half):
        for ii in range(n_i1):
            if pending and (hf, ii) != (0, 0):
                pending.pop(0)()
            gate_group(hf, ii)
        pending.extend(functools.partial(out_piece, hf, r) for r in range(D_MODEL // _MM_ROWS))
    for piece in pending:
        piece()

    @pl.when(j == pl.num_programs(1) - 1)
    def _():
        x2 = x1_ref[...] + acc_ref[...].T
        y_ref[...] = x2 * lax.rsqrt(jnp.mean(x2 * x2, axis=-1, keepdims=True) + EPS) * gf_ref[...]


def _peer_dense(h2T, u_bf, vT_bf, s1T, s2T, thr, shift, x1, lnf_g, tm):
    n = x1.shape[0]
    assert n % tm == 0
    ne = u_bf.shape[0] // _EXPERT_BLOCK
    gf = lnf_g.reshape(1, D_MODEL)
    sc = pl.BlockSpec((PEER_HEADS, N_KEYS, tm), lambda i, j: (0, 0, i))
    st = pl.BlockSpec((PEER_HEADS, 1, tm), lambda i, j: (0, 0, i))
    return pl.pallas_call(
        functools.partial(_peer_dense_kernel, tm=tm),
        grid=(n // tm, ne),
        in_specs=[pl.BlockSpec((D_MODEL, tm), lambda i, j: (0, i)),
                  pl.BlockSpec((_EXPERT_BLOCK, D_MODEL), lambda i, j: (j, 0)),
                  pl.BlockSpec((D_MODEL, _EXPERT_BLOCK), lambda i, j: (0, j)),
                  sc, sc, st, st,
                  pl.BlockSpec((tm, D_MODEL), lambda i, j: (i, 0)),
                  pl.BlockSpec((1, D_MODEL), lambda i, j: (0, 0))],
        out_specs=pl.BlockSpec((tm, D_MODEL), lambda i, j: (i, 0)),
        out_shape=jax.ShapeDtypeStruct((n, D_MODEL), F32),
        scratch_shapes=[pltpu.VMEM((_EXPERT_BLOCK, tm), F32),
                        pltpu.VMEM((_EXPERT_BLOCK, tm), BF16),
                        pltpu.VMEM((D_MODEL, tm), F32)],
        compiler_params=_params("parallel", "arbitrary"),
        name="peer_dense",
    )(h2T, u_bf, vT_bf, s1T, s2T, thr, shift, x1, gf)


def _mix_to_output(x2d, oaT, ob, lw, tm_out, tl_sel, tm_dense):
    x1, h2T, s1T, s2T = _outproj(x2d, oaT, ob, lw["woa"], lw["wob"], lw["ln2_g"], lw["wq"],
                                 lw["k1"], lw["k2"], tm_out)
    thr, shift = _peer_select(s1T, s2T, tl_sel)
    return _peer_dense(h2T, lw["u"], lw["vT"], s1T, s2T, thr, shift, x1, lw["lnf_g"], tm_dense)


def _layer_prompt(x, lw):
    nb, t_len, _ = x.shape
    n = nb * t_len
    x2d = x.reshape(n, D_MODEL)
    ka, va, ki, qb, kb, vb, gb, qaT, qiT, vaT, wiT = _inproj(x2d, lw["ln1_g"], lw["wstd"], lw["wt"], 512)
    tq = 256
    nj = t_len // tq
    variants = tuple((tq * (j + 1), tq * j, tq * (j + 1)) for j in range(nj))
    oaT = _dsa(qaT, qiT, wiT, ka.reshape(nb, t_len, 128), ki.reshape(nb, t_len, 64), vaT,
               nb=nb, tq=tq, variants=variants, topk=min(TOPK_MAX, t_len // 4))
    s0 = jnp.zeros((nb, 2 * LANE, DV_B), F32)
    ob, s_fin = _retention(qb, kb, vb, gb, s0, jnp.arange(t_len, dtype=I32),
                           nb=nb, t_len=t_len, c_len=CHUNK, nsub=4)
    y = _mix_to_output(x2d, oaT, ob, lw, 512, 256, 512)
    return (y.reshape(nb, t_len, D_MODEL), ka.reshape(nb, t_len, KVH_A, DH_A),
            va.reshape(nb, t_len, KVH_A, DH_A), ki.reshape(nb, t_len, D_IDX), _state_from_packed(s_fin))


def _layer_sample(x, ck, cv, cki, cs, lw):
    nb, t_len, _ = x.shape
    past = ck.shape[1]
    n = nb * t_len
    x2d = x.reshape(n, D_MODEL)
    ka, va, ki, qb, kb, vb, gb, qaT, qiT, vaT, wiT = _inproj(x2d, lw["ln1_g"], lw["wstd"], lw["wt"], n)
    total = past + t_len
    s_pad = -(-total // LANE) * LANE
    padk = lambda a: jnp.pad(a, ((0, 0), (0, s_pad - total), (0, 0)))
    k_all = padk(jnp.concatenate([ck.reshape(nb, past, 128), ka.reshape(nb, t_len, 128)], axis=1))
    v_all = padk(jnp.concatenate([cv.reshape(nb, past, 128), va.reshape(nb, t_len, 128)], axis=1))
    ki_all = padk(jnp.concatenate([cki, ki.reshape(nb, t_len, D_IDX)], axis=1))
    per_head = lambda a, d: a.reshape(H_A, d, nb, t_len).transpose(2, 0, 3, 1)
    qa = per_head(qaT, DH_A) * (DH_A ** -0.5)
    group_cols = (jnp.arange(KVH_A * DH_A) // DH_A)[None, :] == (jnp.arange(H_A) // GRP_A)[:, None]
    qa_rows = jnp.where(group_cols[None, :, None, :], jnp.concatenate([qa] * KVH_A, axis=-1), 0.0)
    oa = _dsa_few(qa_rows.reshape(nb, H_A * t_len, KVH_A * DH_A),
                  per_head(qiT, D_IDX).reshape(nb, H_IDX * t_len, D_IDX),
                  wiT.reshape(H_IDX, nb, t_len).transpose(1, 2, 0), k_all, ki_all, v_all,
                  nq=t_len, qpos0=past, n_valid=total, topk=min(TOPK_MAX, total // 4))
    oaT = oa.T
    ob, s_new = _retention(qb, kb, vb, gb, _state_to_packed(cs), past + jnp.arange(t_len, dtype=I32),
                           nb=nb, t_len=t_len, c_len=t_len, nsub=1)
    y = _mix_to_output(x2d, oaT, ob, lw, n, n, n)
    return (y.reshape(nb, t_len, D_MODEL), ka.reshape(nb, t_len, KVH_A, DH_A),
            va.reshape(nb, t_len, KVH_A, DH_A), ki.reshape(nb, t_len, D_IDX), _state_from_packed(s_new))


def kernel(x_prompt, x_sample, cache_attn_k, cache_attn_v, cache_idx_k, state_retention,
           ln1_g, w_in, w_out, ln2_g, peer_w_query, peer_keys1, peer_keys2, peer_u, peer_v, ln_final_g):
    depth = w_in.shape[0]
    assert depth == 1, "the final norm is fused into the (single) layer's last kernel"
    wstd, wt = _prep_inproj_weights(w_in[0])
    lw = dict(ln1_g=ln1_g[0], wstd=wstd, wt=wt,
              woa=w_out[0, :W_A].astype(BF16), wob=w_out[0, W_A:].astype(BF16),
              ln2_g=ln2_g[0], wq=peer_w_query[0].astype(BF16),
              k1=peer_keys1[0], k2=peer_keys2[0],
              u=peer_u[0].astype(BF16), vT=peer_v[0].T.astype(BF16), lnf_g=ln_final_g)
    yp, kp, vp, kip, sp = _layer_prompt(x_prompt, lw)
    ys, ks, vs, kis, ss = _layer_sample(x_sample, cache_attn_k[0], cache_attn_v[0], cache_idx_k[0],
                                        state_retention[0], lw)
    st = lambda a: a[None]
    return (yp, ys, st(kp), st(vp), st(kip), st(sp), st(ks), st(vs), st(kis), st(ss))
```

```python
import functools

import jax
import jax.numpy as jnp
from jax import lax
from jax.experimental import pallas as pl
from jax.experimental.pallas import tpu as pltpu

F32 = jnp.float32
BF16 = jnp.bfloat16
I32 = jnp.int32

D_MODEL = 1024
CHUNK = 64
EPS = 1e-6
H_A, KVH_A, DH_A = 8, 2, 64
GRP_A = H_A // KVH_A
H_IDX, D_IDX = 8, 64
TOPK_MAX = 256
H_B, DK_B, DV_B = 4, 64, 128
ROT_BASE = 10000.0
W_A = H_A * DH_A
W_B = H_B * DV_B
N_KEYS = 128
PEER_HEADS = 8
PEER_TOPK = 16
D_PHALF = 128

LOG2E = 1.4426950408889634
INT_MIN = -(2 ** 31)
NEG_BIG = -1e30
MASKED_SCORE = -3.0e38
LANE = 128
VMEM_LIMIT_BYTES = 56 * 1024 * 1024

_NT = (((1,), (1,)), ((), ()))
_TN = (((0,), (0,)), ((), ()))


def _bdot(a, b, dims=None):
    a = a.astype(BF16)
    b = b.astype(BF16)
    if dims is None:
        return jnp.dot(a, b, preferred_element_type=F32)
    return lax.dot_general(a, b, dims, preferred_element_type=F32)


def _params(*sem):
    return pltpu.CompilerParams(dimension_semantics=sem, vmem_limit_bytes=VMEM_LIMIT_BYTES)


def _key_float(k):
    return pltpu.bitcast(k ^ ((k >> 31) & 0x7FFFFFFF), F32)


def _kth_largest_key(count_ge, k):
    kf = float(k)
    t0 = jnp.where(count_ge(0.0) >= kf, 0, INT_MIN).astype(I32)

    def bit_body(i, t):
        cand = t | lax.shift_left(jnp.int32(1), 30 - i)
        return jnp.where(count_ge(_key_float(cand)) >= kf, cand, t)

    return lax.fori_loop(0, 31, bit_body, t0)


def _inproj_kernel(x_ref, g_ref, wstd_ref, wt_ref,
                   ka_ref, va_ref, ki_ref, qb_ref, kb_ref, vb_ref, gb_ref,
                   qaT_ref, qiT_ref, vaT_ref, wiT_ref):
    x = x_ref[...]
    h = x * lax.rsqrt(jnp.mean(x * x, axis=-1, keepdims=True) + EPS) * g_ref[...]
    hb = h.astype(BF16)
    std = jnp.dot(hb, wstd_ref[...], preferred_element_type=F32)
    ka_ref[...] = std[:, 0:128]
    va_ref[...] = std[:, 128:256]
    ki_ref[...] = std[:, 256:320]
    qb_ref[...] = std[:, 384:640]
    kb_ref[...] = std[:, 640:896]
    vb_ref[...] = std[:, 896:1408]
    gb_ref[...] = std[:, 1408:1920]
    t = lax.dot_general(wt_ref[...], hb, _NT, preferred_element_type=F32)
    qaT_ref[...] = t[0:512]
    qiT_ref[...] = t[512:1024]
    vaT_ref[...] = t[1024:1152]
    wiT_ref[...] = t[1152:1160]


def _retention_perm():
    idx = []
    for half in range(2):
        for h in range(H_B):
            for d in range(DK_B // 2):
                idx.append(h * DK_B + half * (DK_B // 2) + d)
    return jnp.asarray(idx, dtype=I32)


def _prep_inproj_weights(w_in):
    o = [0]
    for s in (W_A, KVH_A * DH_A, KVH_A * DH_A, H_IDX * D_IDX, D_IDX, H_IDX,
              H_B * DK_B, H_B * DK_B, W_B, W_B):
        o.append(o[-1] + s)
    q_a, k_a, v_a, q_i, k_i, w_i, q_b, k_b, v_b, g_b = [w_in[:, o[i]:o[i + 1]] for i in range(10)]
    perm = _retention_perm()
    pad = jnp.zeros((D_MODEL, 64), w_in.dtype)
    wstd = jnp.concatenate([k_a, v_a, k_i, pad, q_b[:, perm], k_b[:, perm], v_b, g_b], axis=1)
    wt = jnp.concatenate([q_a, q_i, v_a, w_i], axis=1).T
    return wstd.astype(BF16), wt.astype(BF16)


def _inproj(x2d, ln_g, wstd, wt, tm):
    n = x2d.shape[0]
    assert n % tm == 0
    row = lambda c: pl.BlockSpec((tm, c), lambda i: (i, 0))
    col = lambda r: pl.BlockSpec((r, tm), lambda i: (0, i))
    full = lambda a: pl.BlockSpec(a.shape, lambda i: (0,) * a.ndim)
    g2 = ln_g.reshape(1, D_MODEL)
    out_shape = (
        jax.ShapeDtypeStruct((n, 128), F32), jax.ShapeDtypeStruct((n, 128), F32),
        jax.ShapeDtypeStruct((n, 64), F32),
        jax.ShapeDtypeStruct((n, 256), F32), jax.ShapeDtypeStruct((n, 256), F32),
        jax.ShapeDtypeStruct((n, 512), F32), jax.ShapeDtypeStruct((n, 512), F32),
        jax.ShapeDtypeStruct((512, n), F32), jax.ShapeDtypeStruct((512, n), F32),
        jax.ShapeDtypeStruct((128, n), F32), jax.ShapeDtypeStruct((8, n), F32),
    )
    out_specs = (row(128), row(128), row(64), row(256), row(256), row(512), row(512),
                 col(512), col(512), col(128), col(8))
    return pl.pallas_call(
        _inproj_kernel,
        grid=(n // tm,),
        in_specs=[row(D_MODEL), full(g2), full(wstd), full(wt)],
        out_specs=out_specs,
        out_shape=out_shape,
        compiler_params=_params("parallel"),
        name="inproj",
    )(x2d, g2, wstd, wt)


_KEY_ROWS = 128


def _dsa_body(qaT_ref, qiT_ref, wiT_ref, ka_ref, ki_ref, vaT_ref, oT_ref, score_ref, bias_ref,
              qpad_ref, s_ref, p_ref, m_ref, l_ref,
              *, n_keys, tq, qpos0, n_valid, topk):
    ch = _KEY_ROWS
    nch = n_keys // ch
    row_iota = lax.broadcasted_iota(I32, (ch, tq), 0)
    lane_iota = lax.broadcasted_iota(I32, (ch, tq), 1)
    q_chunk = (qpos0 + lane_iota) // CHUNK
    wi = wiT_ref[...]

    def allowed(srow):
        return ((srow // CHUNK) <= q_chunk) & (srow < n_valid)

    qi = [qiT_ref[D_IDX * h:D_IDX * (h + 1), :].astype(BF16) for h in range(H_IDX)]

    def score_chunk(c, carry):
        r0 = pl.multiple_of(c * ch, ch)
        kic = ki_ref[pl.ds(r0, ch), :].astype(BF16)
        acc = jnp.zeros((ch, tq), F32)
        for h in range(H_IDX):
            d = jnp.dot(kic, qi[h], preferred_element_type=F32)
            acc = acc + wi[h:h + 1, :] * jnp.maximum(d, 0.0)
        score_ref[pl.ds(r0, ch), :] = jnp.where(allowed(r0 + row_iota), acc, MASKED_SCORE)
        return carry

    lax.fori_loop(0, nch, score_chunk, 0)

    def count(pred):
        def body(c, acc):
            r0 = pl.multiple_of(c * ch, ch)
            x = score_ref[pl.ds(r0, ch), :]
            hit = jnp.where(pred(x, r0 + row_iota), 1.0, 0.0)
            return acc + jnp.sum(hit.reshape(ch // 8, 8, tq), axis=0)
        acc = lax.fori_loop(0, nch, body, jnp.zeros((8, tq), F32), unroll=min(nch, 4))
        return jnp.sum(acc, axis=0, keepdims=True)

    t = _kth_largest_key(lambda piv: count(lambda x, r: x >= piv), topk)
    lo = _key_float(t)
    hi = _key_float(t + 1)
    need = float(topk) - count(lambda x, r: x >= hi)

    tri = jnp.where(lax.broadcasted_iota(I32, (ch, ch), 0) > lax.broadcasted_iota(I32, (ch, ch), 1),
                    1.0, 0.0).astype(BF16)

    def bias_chunk(c, ties_before):
        r0 = pl.multiple_of(c * ch, ch)
        x = score_ref[pl.ds(r0, ch), :]
        tie = (x >= lo) & (x < hi)
        tie_f = jnp.where(tie, 1.0, 0.0)
        before = ties_before + jnp.dot(tri, tie_f.astype(BF16), preferred_element_type=F32)
        sel = ((x >= hi) | (tie & (before < need))) & allowed(r0 + row_iota)
        bias_ref[pl.ds(r0, ch), :] = jnp.where(sel, 0.0, NEG_BIG)
        return ties_before + jnp.sum(tie_f, axis=0, keepdims=True)

    lax.fori_loop(0, nch, bias_chunk, jnp.zeros((1, tq), F32))

    zeros_half = jnp.zeros((DH_A, tq), F32)
    scale = DH_A ** -0.5
    for h in range(H_A):
        qh = qaT_ref[DH_A * h:DH_A * (h + 1), :] * scale
        qpad = (jnp.concatenate([qh, zeros_half], axis=0) if h // GRP_A == 0
                else jnp.concatenate([zeros_half, qh], axis=0))
        qpad_ref[:, tq * h:tq * (h + 1)] = qpad.astype(BF16)
    m_ref[...] = jnp.full(m_ref.shape, NEG_BIG, F32)
    l_ref[...] = jnp.zeros(l_ref.shape, F32)
    oT_ref[...] = jnp.zeros(oT_ref.shape, F32)

    def att_chunk(c, carry):
        r0 = pl.multiple_of(c * ch, ch)
        kc = ka_ref[pl.ds(r0, ch), :].astype(BF16)
        bias = bias_ref[pl.ds(r0, ch), :]
        s_ref[...] = jnp.dot(kc, qpad_ref[...], preferred_element_type=F32)
        alphas = []
        for h in range(H_A):
            lanes = slice(tq * h, tq * (h + 1))
            s = s_ref[:, lanes] + bias
            m_old = m_ref[h:h + 1, :]
            m_new = jnp.maximum(m_old, jnp.max(s, axis=0, keepdims=True))
            alpha = jnp.exp(m_old - m_new)
            p = jnp.exp(s - m_new)
            l_ref[h:h + 1, :] = l_ref[h:h + 1, :] * alpha + jnp.sum(p, axis=0, keepdims=True)
            p_ref[:, lanes] = p.astype(BF16)
            m_ref[h:h + 1, :] = m_new
            alphas.append(alpha)
        for g in range(KVH_A):
            vc = vaT_ref[DH_A * g:DH_A * (g + 1), pl.ds(r0, ch)].astype(BF16)
            pv = jnp.dot(vc, p_ref[:, GRP_A * tq * g:GRP_A * tq * (g + 1)],
                         preferred_element_type=F32)
            for hh in range(GRP_A):
                h = GRP_A * g + hh
                rows = slice(DH_A * h, DH_A * (h + 1))
                oT_ref[rows, :] = oT_ref[rows, :] * alphas[h] + pv[:, tq * hh:tq * (hh + 1)]
        return carry

    lax.fori_loop(0, nch, att_chunk, 0)
    for h in range(H_A):
        rows = slice(DH_A * h, DH_A * (h + 1))
        oT_ref[rows, :] = oT_ref[rows, :] / l_ref[h:h + 1, :]


def _dsa_kernel(qaT_ref, qiT_ref, wiT_ref, ka_ref, ki_ref, vaT_ref, oT_ref, score_ref, bias_ref,
              qpad_ref, s_ref, p_ref, m_ref, l_ref,
                *, tq, variants, topk):
    j = pl.program_id(1)
    for jj, (n_keys, qpos0, n_valid) in enumerate(variants):
        @pl.when(j == jj)
        def _(n_keys=n_keys, qpos0=qpos0, n_valid=n_valid):
            _dsa_body(qaT_ref, qiT_ref, wiT_ref, ka_ref, ki_ref, vaT_ref, oT_ref, score_ref, bias_ref,
              qpad_ref, s_ref, p_ref, m_ref, l_ref,
                      n_keys=n_keys, tq=tq, qpos0=qpos0, n_valid=n_valid, topk=topk)


def _dsa(qaT, qiT, wiT, ka, ki, vaT, *, nb, tq, variants, topk):
    nj = len(variants)
    s_max = ka.shape[1]
    qspec = lambda r: pl.BlockSpec((r, tq), lambda b, j: (0, b * nj + j))
    return pl.pallas_call(
        functools.partial(_dsa_kernel, tq=tq, variants=variants, topk=topk),
        grid=(nb, nj),
        in_specs=[qspec(512), qspec(512), qspec(8),
                  pl.BlockSpec((None, s_max, 128), lambda b, j: (b, 0, 0)),
                  pl.BlockSpec((None, s_max, 64), lambda b, j: (b, 0, 0)),
                  pl.BlockSpec((128, s_max), lambda b, j: (0, b))],
        out_specs=qspec(512),
        out_shape=jax.ShapeDtypeStruct((512, nb * nj * tq), F32),
        scratch_shapes=[pltpu.VMEM((s_max, tq), F32), pltpu.VMEM((s_max, tq), F32),
                        pltpu.VMEM((KVH_A * DH_A, H_A * tq), BF16),
                        pltpu.VMEM((_KEY_ROWS, H_A * tq), F32),
                        pltpu.VMEM((_KEY_ROWS, H_A * tq), BF16),
                        pltpu.VMEM((H_A, tq), F32), pltpu.VMEM((H_A, tq), F32)],
        compiler_params=_params("parallel", "arbitrary"),
        name="dsa",
    )(qaT, qiT, wiT, ka, ki, vaT)


def _dsa_few_kernel(qa_ref, qi_ref, wi_ref, ka_ref, ki_ref, va_ref, o_ref, *, nq, n_keys, qpos0, n_valid, topk):
    kib = ki_ref[...].astype(BF16)
    dots = lax.dot_general(qi_ref[...].astype(BF16), kib, _NT, preferred_element_type=F32)
    wi = wi_ref[...]
    score = jnp.zeros((nq, n_keys), F32)
    for h in range(H_IDX):
        score = score + wi[:, h:h + 1] * jnp.maximum(dots[nq * h:nq * (h + 1), :], 0.0)
    key_pos = lax.broadcasted_iota(I32, (nq, n_keys), 1)
    q_pos = qpos0 + lax.broadcasted_iota(I32, (nq, n_keys), 0)
    allowed = ((key_pos // CHUNK) <= (q_pos // CHUNK)) & (key_pos < n_valid)
    x = jnp.where(allowed, score, MASKED_SCORE)

    def count(pred):
        return jnp.sum(jnp.where(pred, 1.0, 0.0), axis=1, keepdims=True)

    t = _kth_largest_key(lambda piv: count(x >= piv), topk)
    lo = _key_float(t)
    hi = _key_float(t + 1)
    need = float(topk) - count(x >= hi)
    tie = (x >= lo) & (x < hi)
    tie_f = jnp.where(tie, 1.0, 0.0)
    tri = jnp.where(lax.broadcasted_iota(I32, (LANE, LANE), 0) < lax.broadcasted_iota(I32, (LANE, LANE), 1),
                    1.0, 0.0).astype(BF16)
    running = jnp.zeros((nq, 1), F32)
    before = []
    for c in range(n_keys // LANE):
        blk = tie_f[:, LANE * c:LANE * (c + 1)]
        before.append(running + jnp.dot(blk.astype(BF16), tri, preferred_element_type=F32))
        running = running + jnp.sum(blk, axis=1, keepdims=True)
    before = jnp.concatenate(before, axis=1)
    sel = ((x >= hi) | (tie & (before < need))) & allowed
    bias = jnp.where(sel, 0.0, NEG_BIG)

    kab = ka_ref[...].astype(BF16)
    s = lax.dot_general(qa_ref[...].astype(BF16), kab, _NT, preferred_element_type=F32)
    s = s + jnp.concatenate([bias] * H_A, axis=0)
    m = jnp.max(s, axis=1, keepdims=True)
    p = jnp.exp(s - m)
    l = jnp.sum(p, axis=1, keepdims=True)
    pv = jnp.dot(p.astype(BF16), va_ref[...].astype(BF16), preferred_element_type=F32) / l
    for h in range(H_A):
        g = h // GRP_A
        o_ref[:, DH_A * h:DH_A * (h + 1)] = pv[nq * h:nq * (h + 1), DH_A * g:DH_A * (g + 1)]


def _dsa_few(qa_rows, qi_rows, wi, ka, ki, va, *, nq, qpos0, n_valid, topk):
    nb, n_keys = ka.shape[0], ka.shape[1]
    blk = lambda r, c: pl.BlockSpec((None, r, c), lambda b: (b, 0, 0))
    return pl.pallas_call(
        functools.partial(_dsa_few_kernel, nq=nq, n_keys=n_keys, qpos0=qpos0, n_valid=n_valid, topk=topk),
        grid=(nb,),
        in_specs=[blk(H_A * nq, 128), blk(H_IDX * nq, 64), blk(nq, H_IDX),
                  blk(n_keys, 128), blk(n_keys, 64), blk(n_keys, 128)],
        out_specs=pl.BlockSpec((nq, W_A), lambda b: (b, 0)),
        out_shape=jax.ShapeDtypeStruct((nb * nq, W_A), F32),
        compiler_params=_params("parallel"),
        name="dsa_few",
    )(qa_rows, qi_rows, wi, ka, ki, va)


def _ret_kernel(q_ref, k_ref, v_ref, g_ref, cos_ref, sin_ref, dmask_ref, rdec_ref, kdec_ref,
                sdec_ref, s0_ref, o_ref, sout_ref, state_ref, *, c_len, nsub):
    ci = pl.program_id(1)

    @pl.when(ci == 0)
    def _():
        state_ref[...] = s0_ref[...]

    lane_head = (lax.broadcasted_iota(I32, (1, 2 * LANE), 1) % LANE) // (DK_B // 2)
    row_head = (lax.broadcasted_iota(I32, (2 * LANE, DV_B), 0) % LANE) // (DK_B // 2)
    for sub in range(nsub):
        rows = slice(sub * c_len, (sub + 1) * c_len)
        q = q_ref[rows, :]
        k = k_ref[rows, :]
        v = v_ref[rows, :]
        g = g_ref[rows, :]
        cos = cos_ref[rows, :]
        sin = sin_ref[rows, :]
        q1, q2 = q[:, :LANE], q[:, LANE:]
        k1, k2 = k[:, :LANE], k[:, LANE:]
        qr = jnp.concatenate([q1 * cos - q2 * sin, q1 * sin + q2 * cos], axis=1)
        kr = jnp.concatenate([k1 * cos - k2 * sin, k1 * sin + k2 * cos], axis=1) * (DK_B ** -0.5)
        state = state_ref[...]
        upd = _bdot(kr * kdec_ref[...], v, _TN)
        new_state = sdec_ref[...] * state
        vb = v.astype(BF16)
        krb = kr.astype(BF16)
        sb = state.astype(BF16)
        for h in range(H_B):
            qm = jnp.where(lane_head == h, qr, 0.0).astype(BF16)
            qk = lax.dot_general(qm, krb, _NT, preferred_element_type=F32) * dmask_ref[h]
            intra = jnp.dot(qk.astype(BF16), vb[:, DV_B * h:DV_B * (h + 1)], preferred_element_type=F32)
            inter = jnp.dot(qm, sb, preferred_element_type=F32) * rdec_ref[h]
            o = intra + inter
            mu = jnp.mean(o, axis=-1, keepdims=True)
            d = o - mu
            var = jnp.mean(d * d, axis=-1, keepdims=True)
            gh = g[:, DV_B * h:DV_B * (h + 1)]
            silu = gh * (1.0 / (1.0 + jnp.exp(-gh)))
            o_ref[rows, DV_B * h:DV_B * (h + 1)] = silu * (d * lax.rsqrt(var + EPS))
            new_state = new_state + jnp.where(row_head == h, upd[:, DV_B * h:DV_B * (h + 1)], 0.0)
        state_ref[...] = new_state

    @pl.when(ci == pl.num_programs(1) - 1)
    def _():
        sout_ref[...] = state_ref[...]


def _retention_tables(pos, c_len):
    half = DK_B // 2
    lg = jnp.log(1.0 - 2.0 ** (-5.0 - jnp.arange(H_B, dtype=F32)))
    inv = 1.0 / (ROT_BASE ** jnp.linspace(0.0, 1.0, half, dtype=F32))
    ang = pos.astype(F32)[:, None] * inv[None, :]
    cos = jnp.tile(jnp.cos(ang), (1, H_B))
    sin = jnp.tile(jnp.sin(ang), (1, H_B))
    i = jnp.arange(c_len, dtype=F32)
    diff = i[:, None] - i[None, :]
    dmask = jnp.where(diff >= 0, jnp.exp(jnp.maximum(diff, 0.0)[None] * lg[:, None, None]), 0.0)
    rdec = jnp.broadcast_to(jnp.exp((i + 1.0)[None, :] * lg[:, None])[:, :, None], (H_B, c_len, DV_B))
    lane_head = (jnp.arange(2 * LANE) % LANE) // half
    kdec = jnp.exp((c_len - 1.0 - i)[:, None] * lg[lane_head][None, :])
    sdec = jnp.broadcast_to(jnp.exp(c_len * lg[lane_head])[:, None], (2 * LANE, DV_B))
    return cos, sin, dmask.astype(F32), rdec.astype(F32), kdec.astype(F32), sdec.astype(F32)


def _state_to_packed(s):
    b = s.shape[0]
    return s.reshape(b, H_B, 2, DK_B // 2, DV_B).transpose(0, 2, 1, 3, 4).reshape(b, 2 * LANE, DV_B)


def _state_from_packed(s):
    b = s.shape[0]
    return s.reshape(b, 2, H_B, DK_B // 2, DV_B).transpose(0, 2, 1, 3, 4).reshape(b, H_B, DK_B, DV_B)


def _retention(qb, kb, vb, gb, s0_packed, pos, *, nb, t_len, c_len, nsub):
    blk = c_len * nsub
    nc = t_len // blk
    cos, sin, dmask, rdec, kdec, sdec = _retention_tables(pos, c_len)
    tok = lambda c: pl.BlockSpec((blk, c), lambda b, i: (b * nc + i, 0))
    tab = pl.BlockSpec((blk, LANE), lambda b, i: (i, 0))
    full = lambda a: pl.BlockSpec(a.shape, lambda b, i: (0,) * a.ndim)
    st = pl.BlockSpec((None, 2 * LANE, DV_B), lambda b, i: (b, 0, 0))
    return pl.pallas_call(
        functools.partial(_ret_kernel, c_len=c_len, nsub=nsub),
        grid=(nb, nc),
        in_specs=[tok(256), tok(256), tok(512), tok(512), tab, tab,
                  full(dmask), full(rdec), full(kdec), full(sdec), st],
        out_specs=(tok(512), st),
        out_shape=(jax.ShapeDtypeStruct((nb * t_len, W_B), F32),
                   jax.ShapeDtypeStruct((nb, 2 * LANE, DV_B), F32)),
        scratch_shapes=[pltpu.VMEM((2 * LANE, DV_B), F32)],
        compiler_params=_params("parallel", "arbitrary"),
        name="retention",
    )(qb, kb, vb, gb, cos, sin, dmask, rdec, kdec, sdec, s0_packed)


def _outproj_kernel(x_ref, oaT_ref, ob_ref, woa_ref, wob_ref, g2_ref, wq_ref, k1_ref, k2_ref,
                    x1_ref, h2T_ref, s1T_ref, s2T_ref):
    mix = _bdot(oaT_ref[...], woa_ref[...], _TN) + _bdot(ob_ref[...], wob_ref[...])
    x1 = x_ref[...] + mix
    x1_ref[...] = x1
    h2 = x1 * lax.rsqrt(jnp.mean(x1 * x1, axis=-1, keepdims=True) + EPS) * g2_ref[...]
    h2T_ref[...] = h2.T.astype(BF16)
    q = _bdot(h2, wq_ref[...])
    k1 = k1_ref[...]
    k2 = k2_ref[...]
    for h in range(PEER_HEADS):
        base = 2 * D_PHALF * h
        s1T_ref[h] = _bdot(k1, q[:, base:base + D_PHALF], _NT) * LOG2E
        s2T_ref[h] = _bdot(k2, q[:, base + D_PHALF:base + 2 * D_PHALF], _NT) * LOG2E


def _outproj(x2d, oaT, ob, woa, wob, ln2_g, wq, k1, k2, tm):
    n = x2d.shape[0]
    assert n % tm == 0
    g2 = ln2_g.reshape(1, D_MODEL)
    row = lambda c: pl.BlockSpec((tm, c), lambda i: (i, 0))
    col = lambda r: pl.BlockSpec((r, tm), lambda i: (0, i))
    full = lambda a: pl.BlockSpec(a.shape, lambda i: (0,) * a.ndim)
    sc = pl.BlockSpec((PEER_HEADS, N_KEYS, tm), lambda i: (0, 0, i))
    return pl.pallas_call(
        _outproj_kernel,
        grid=(n // tm,),
        in_specs=[row(D_MODEL), col(W_A), row(W_B), full(woa), full(wob), full(g2), full(wq),
                  full(k1), full(k2)],
        out_specs=(row(D_MODEL), col(D_MODEL), sc, sc),
        out_shape=(jax.ShapeDtypeStruct((n, D_MODEL), F32),
                   jax.ShapeDtypeStruct((D_MODEL, n), BF16),
                   jax.ShapeDtypeStruct((PEER_HEADS, N_KEYS, n), F32),
                   jax.ShapeDtypeStruct((PEER_HEADS, N_KEYS, n), F32)),
        compiler_params=_params("parallel"),
        name="outproj",
    )(x2d, oaT, ob, woa, wob, g2, wq, k1, k2)


def _peer_sel_kernel(s1T_ref, s2T_ref, thr_ref, shift_ref, a_ref, b_ref, cand_ref):
    def top16(s, dst_ref):
        for r in range(PEER_TOPK):
            m = jnp.max(s, axis=0, keepdims=True)
            dst_ref[r:r + 1, :] = m
            s = jnp.where(s == m, -jnp.inf, s)

    def candidates(h, carry):
        top16(s1T_ref[h], a_ref)
        top16(s2T_ref[h], b_ref)
        a = a_ref[...]
        b = b_ref[...]
        cand = [a + b[0:1]] + [a[0:8] + b[j:j + 1] for j in range(1, 8)] + [a[0:1] + b[8:16]]
        cand_ref[h] = jnp.concatenate(cand, axis=0)
        return carry

    lax.fori_loop(0, PEER_HEADS, candidates, 0)

    def count_ge(piv):
        rows = []
        for h in range(PEER_HEADS):
            p = piv if jnp.ndim(piv) == 0 else piv[h:h + 1, :]
            rows.append(jnp.sum(jnp.where(cand_ref[h] >= p, 1.0, 0.0), axis=0, keepdims=True))
        return jnp.concatenate(rows, axis=0)

    thr_all = _key_float(_kth_largest_key(count_ge, PEER_TOPK))
    for h in range(PEER_HEADS):
        cand = cand_ref[h]
        thr = thr_all[h:h + 1, :]
        top = cand[0:1]
        z = jnp.sum(jnp.where(cand >= thr, jnp.exp2(cand - top), 0.0), axis=0, keepdims=True)
        thr_ref[h] = thr
        shift_ref[h] = top + jnp.log2(z)


def _peer_select(s1T, s2T, tl):
    n = s1T.shape[2]
    assert n % tl == 0
    sc = pl.BlockSpec((PEER_HEADS, N_KEYS, tl), lambda i: (0, 0, i))
    st = pl.BlockSpec((PEER_HEADS, 1, tl), lambda i: (0, 0, i))
    stat = jax.ShapeDtypeStruct((PEER_HEADS, 1, n), F32)
    return pl.pallas_call(
        _peer_sel_kernel,
        grid=(n // tl,),
        in_specs=[sc, sc],
        out_specs=(st, st),
        out_shape=(stat, stat),
        scratch_shapes=[pltpu.VMEM((PEER_TOPK, tl), F32), pltpu.VMEM((PEER_TOPK, tl), F32),
                        pltpu.VMEM((PEER_HEADS, 80, tl), F32)],
        compiler_params=_params("parallel"),
        name="peer_select",
    )(s1T, s2T)


_I1_PER_BLOCK = 16
_EXPERT_BLOCK = _I1_PER_BLOCK * N_KEYS
_MM_ROWS = 256
_SQRT_HALF = 0.7071067811865476


def _peer_dense_kernel(h2T_ref, u_ref, vT_ref, s1T_ref, s2T_ref, thr_ref, shift_ref,
                       x1_ref, gf_ref, y_ref, aT_ref, wT_ref, acc_ref, *, tm):
    j = pl.program_id(1)
    n_i1 = _I1_PER_BLOCK
    n_lt = tm // LANE
    n_half = 2 if n_lt % 2 == 0 else 1
    lt_per_half = n_lt // n_half
    half_lanes = [slice(hf * lt_per_half * LANE, (hf + 1) * lt_per_half * LANE) for hf in range(n_half)]

    @pl.when(j == 0)
    def _():
        acc_ref[...] = jnp.zeros_like(acc_ref)

    i1_base = pl.multiple_of(j * n_i1, n_i1)

    def act_piece(hf, p):
        rows = slice(_MM_ROWS * p, _MM_ROWS * (p + 1))
        aT_ref[rows, half_lanes[hf]] = jnp.dot(u_ref[rows, :], h2T_ref[:, half_lanes[hf]],
                                               preferred_element_type=F32)

    def out_piece(hf, r):
        rows = slice(_MM_ROWS * r, _MM_ROWS * (r + 1))
        acc_ref[rows, half_lanes[hf]] += jnp.dot(vT_ref[rows, :], wT_ref[:, half_lanes[hf]],
                                                 preferred_element_type=F32)

    def gate_group(hf, ii):
        rows = slice(N_KEYS * ii, N_KEYS * (ii + 1))
        for lt in range(lt_per_half):
            lanes = slice((hf * lt_per_half + lt) * LANE, (hf * lt_per_half + lt + 1) * LANE)
            gate = jnp.zeros((N_KEYS, LANE), F32)
            for h in range(PEER_HEADS):
                s1 = s1T_ref[h, pl.ds(i1_base, n_i1), lanes][ii:ii + 1]
                sm = s1 + s2T_ref[h, :, lanes]
                gate = gate + jnp.where(sm >= thr_ref[h, :, lanes],
                                        jnp.exp2(sm - shift_ref[h, :, lanes]), 0.0)
            a = aT_ref[rows, lanes]
            act = 0.5 * a * (1.0 + lax.erf(a * _SQRT_HALF))
            wT_ref[rows, lanes] = (gate * act).astype(BF16)

    pending = [functools.partial(act_piece, hf, p) for hf in range(n_half)
               for p in range(_EXPERT_BLOCK // _MM_ROWS)]
    pending.pop(0)()
    pending.pop(0)()
    for hf in range(n_half):
        for ii in range(n_i1):
            if pending and (hf, ii) != (0, 0):
                pending.pop(0)()
            gate_group(hf, ii)
        pending.extend(functools.partial(out_piece, hf, r) for r in range(D_MODEL // _MM_ROWS))
    for piece in pending:
        piece()

    @pl.when(j == pl.num_programs(1) - 1)
    def _():
        x2 = x1_ref[...] + acc_ref[...].T
        y_ref[...] = x2 * lax.rsqrt(jnp.mean(x2 * x2, axis=-1, keepdims=True) + EPS) * gf_ref[...]


def _peer_dense(h2T, u_bf, vT_bf, s1T, s2T, thr, shift, x1, lnf_g, tm):
    n = x1.shape[0]
    assert n % tm == 0
    ne = u_bf.shape[0] // _EXPERT_BLOCK
    gf = lnf_g.reshape(1, D_MODEL)
    sc = pl.BlockSpec((PEER_HEADS, N_KEYS, tm), lambda i, j: (0, 0, i))
    st = pl.BlockSpec((PEER_HEADS, 1, tm), lambda i, j: (0, 0, i))
    return pl.pallas_call(
        functools.partial(_peer_dense_kernel, tm=tm),
        grid=(n // tm, ne),
        in_specs=[pl.BlockSpec((D_MODEL, tm), lambda i, j: (0, i)),
                  pl.BlockSpec((_EXPERT_BLOCK, D_MODEL), lambda i, j: (j, 0)),
                  pl.BlockSpec((D_MODEL, _EXPERT_BLOCK), lambda i, j: (0, j)),
                  sc, sc, st, st,
                  pl.BlockSpec((tm, D_MODEL), lambda i, j: (i, 0)),
                  pl.BlockSpec((1, D_MODEL), lambda i, j: (0, 0))],
        out_specs=pl.BlockSpec((tm, D_MODEL), lambda i, j: (i, 0)),
        out_shape=jax.ShapeDtypeStruct((n, D_MODEL), F32),
        scratch_shapes=[pltpu.VMEM((_EXPERT_BLOCK, tm), F32),
                        pltpu.VMEM((_EXPERT_BLOCK, tm), BF16),
                        pltpu.VMEM((D_MODEL, tm), F32)],
        compiler_params=_params("parallel", "arbitrary"),
        name="peer_dense",
    )(h2T, u_bf, vT_bf, s1T, s2T, thr, shift, x1, gf)


def _mix_to_output(x2d, oaT, ob, lw, tm_out, tl_sel, tm_dense):
    x1, h2T, s1T, s2T = _outproj(x2d, oaT, ob, lw["woa"], lw["wob"], lw["ln2_g"], lw["wq"],
                                 lw["k1"], lw["k2"], tm_out)
    thr, shift = _peer_select(s1T, s2T, tl_sel)
    return _peer_dense(h2T, lw["u"], lw["vT"], s1T, s2T, thr, shift, x1, lw["lnf_g"], tm_dense)


def _layer_prompt(x, lw):
    nb, t_len, _ = x.shape
    n = nb * t_len
    x2d = x.reshape(n, D_MODEL)
    ka, va, ki, qb, kb, vb, gb, qaT, qiT, vaT, wiT = _inproj(x2d, lw["ln1_g"], lw["wstd"], lw["wt"], 512)
    tq = 256
    nj = t_len // tq
    variants = tuple((tq * (j + 1), tq * j, tq * (j + 1)) for j in range(nj))
    oaT = _dsa(qaT, qiT, wiT, ka.reshape(nb, t_len, 128), ki.reshape(nb, t_len, 64), vaT,
               nb=nb, tq=tq, variants=variants, topk=min(TOPK_MAX, t_len // 4))
    s0 = jnp.zeros((nb, 2 * LANE, DV_B), F32)
    ob, s_fin = _retention(qb, kb, vb, gb, s0, jnp.arange(t_len, dtype=I32),
                           nb=nb, t_len=t_len, c_len=CHUNK, nsub=4)
    y = _mix_to_output(x2d, oaT, ob, lw, 512, 256, 512)
    return (y.reshape(nb, t_len, D_MODEL), ka.reshape(nb, t_len, KVH_A, DH_A),
            va.reshape(nb, t_len, KVH_A, DH_A), ki.reshape(nb, t_len, D_IDX), _state_from_packed(s_fin))


def _layer_sample(x, ck, cv, cki, cs, lw):
    nb, t_len, _ = x.shape
    past = ck.shape[1]
    n = nb * t_len
    x2d = x.reshape(n, D_MODEL)
    ka, va, ki, qb, kb, vb, gb, qaT, qiT, vaT, wiT = _inproj(x2d, lw["ln1_g"], lw["wstd"], lw["wt"], n)
    total = past + t_len
    s_pad = -(-total // LANE) * LANE
    padk = lambda a: jnp.pad(a, ((0, 0), (0, s_pad - total), (0, 0)))
    k_all = padk(jnp.concatenate([ck.reshape(nb, past, 128), ka.reshape(nb, t_len, 128)], axis=1))
    v_all = padk(jnp.concatenate([cv.reshape(nb, past, 128), va.reshape(nb, t_len, 128)], axis=1))
    ki_all = padk(jnp.concatenate([cki, ki.reshape(nb, t_len, D_IDX)], axis=1))
    per_head = lambda a, d: a.reshape(H_A, d, nb, t_len).transpose(2, 0, 3, 1)
    qa = per_head(qaT, DH_A) * (DH_A ** -0.5)
    group_cols = (jnp.arange(KVH_A * DH_A) // DH_A)[None, :] == (jnp.arange(H_A) // GRP_A)[:, None]
    qa_rows = jnp.where(group_cols[None, :, None, :], jnp.concatenate([qa] * KVH_A, axis=-1), 0.0)
    oa = _dsa_few(qa_rows.reshape(nb, H_A * t_len, KVH_A * DH_A),
                  per_head(qiT, D_IDX).reshape(nb, H_IDX * t_len, D_IDX),
                  wiT.reshape(H_IDX, nb, t_len).transpose(1, 2, 0), k_all, ki_all, v_all,
                  nq=t_len, qpos0=past, n_valid=total, topk=min(TOPK_MAX, total // 4))
    oaT = oa.T
    ob, s_new = _retention(qb, kb, vb, gb, _state_to_packed(cs), past + jnp.arange(t_len, dtype=I32),
                           nb=nb, t_len=t_len, c_len=t_len, nsub=1)
    y = _mix_to_output(x2d, oaT, ob, lw, n, n, n)
    return (y.reshape(nb, t_len, D_MODEL), ka.reshape(nb, t_len, KVH_A, DH_A),
            va.reshape(nb, t_len, KVH_A, DH_A), ki.reshape(nb, t_len, D_IDX), _state_from_packed(s_new))


def kernel(x_prompt, x_sample, cache_attn_k, cache_attn_v, cache_idx_k, state_retention,
           ln1_g, w_in, w_out, ln2_g, peer_w_query, peer_keys1, peer_keys2, peer_u, peer_v, ln_final_g):
    depth = w_in.shape[0]
    assert depth == 1, "the final norm is fused into the (single) layer's last kernel"
    wstd, wt = _prep_inproj_weights(w_in[0])
    lw = dict(ln1_g=ln1_g[0], wstd=wstd, wt=wt,
              woa=w_out[0, :W_A].astype(BF16), wob=w_out[0, W_A:].astype(BF16),
              ln2_g=ln2_g[0], wq=peer_w_query[0].astype(BF16),
              k1=peer_keys1[0], k2=peer_keys2[0],
              u=peer_u[0].astype(BF16), vT=peer_v[0].T.astype(BF16), lnf_g=ln_final_g)
    yp, kp, vp, kip, sp = _layer_prompt(x_prompt, lw)
    ys, ks, vs, kis, ss = _layer_sample(x_sample, cache_attn_k[0], cache_attn_v[0], cache_idx_k[0],
                                        state_retention[0], lw)
    st = lambda a: a[None]
    return (yp, ys, st(kp), st(vp), st(kip), st(sp), st(ks), st(vs), st(kis), st(ss))
```

```python
import functools

import jax
import jax.numpy as jnp
from jax import lax
from jax.experimental import pallas as pl
from jax.experimental.pallas import tpu as pltpu

F32 = jnp.float32
BF16 = jnp.bfloat16
I32 = jnp.int32

D_MODEL = 1024
CHUNK = 64
_CHUNK_SHIFT = 6
assert 1 << _CHUNK_SHIFT == CHUNK
EPS = 1e-6
H_A, KVH_A, DH_A = 8, 2, 64
GRP_A = H_A // KVH_A
H_IDX, D_IDX = 8, 64
TOPK_MAX = 256
H_B, DK_B, DV_B = 4, 64, 128
ROT_BASE = 10000.0
W_A = H_A * DH_A
W_B = H_B * DV_B
N_KEYS = 128
PEER_HEADS = 8
PEER_TOPK = 16
D_PHALF = 128

LOG2E = 1.4426950408889634
INT_MIN = -(2 ** 31)
NEG_BIG = -1e30
MASKED_SCORE = -3.0e38
LANE = 128
VMEM_LIMIT_BYTES = 56 * 1024 * 1024

_NT = (((1,), (1,)), ((), ()))
_TN = (((0,), (0,)), ((), ()))


def _bdot(a, b, dims=None):
    a = a.astype(BF16)
    b = b.astype(BF16)
    if dims is None:
        return jnp.dot(a, b, preferred_element_type=F32)
    return lax.dot_general(a, b, dims, preferred_element_type=F32)


def _params(*sem):
    return pltpu.CompilerParams(dimension_semantics=sem, vmem_limit_bytes=VMEM_LIMIT_BYTES)


def _key_float(k):
    return pltpu.bitcast(k ^ ((k >> 31) & 0x7FFFFFFF), F32)


def _kth_largest_key(count_ge, k):
    kf = float(k)
    t0 = jnp.where(count_ge(0.0) >= kf, 0, INT_MIN).astype(I32)

    def bit_body(i, t):
        cand = t | lax.shift_left(jnp.int32(1), 30 - i)
        return jnp.where(count_ge(_key_float(cand)) >= kf, cand, t)

    return lax.fori_loop(0, 31, bit_body, t0)


def _inproj_kernel(x_ref, g_ref, wstd_ref, wt_ref,
                   ka_ref, va_ref, ki_ref, qb_ref, kb_ref, vb_ref, gb_ref,
                   qaT_ref, qiT_ref, vaT_ref, wiT_ref):
    x = x_ref[...]
    h = x * lax.rsqrt(jnp.mean(x * x, axis=-1, keepdims=True) + EPS) * g_ref[...]
    hb = h.astype(BF16)
    std = jnp.dot(hb, wstd_ref[...], preferred_element_type=F32)
    ka_ref[...] = std[:, 0:128]
    va_ref[...] = std[:, 128:256]
    ki_ref[...] = std[:, 256:320]
    qb_ref[...] = std[:, 384:640]
    kb_ref[...] = std[:, 640:896]
    vb_ref[...] = std[:, 896:1408]
    gb_ref[...] = std[:, 1408:1920]
    t = lax.dot_general(wt_ref[...], hb, _NT, preferred_element_type=F32)
    qaT_ref[...] = t[0:512]
    qiT_ref[...] = t[512:1024]
    vaT_ref[...] = t[1024:1152]
    wiT_ref[...] = t[1152:1160]


def _retention_perm():
    idx = []
    for half in range(2):
        for h in range(H_B):
            for d in range(DK_B // 2):
                idx.append(h * DK_B + half * (DK_B // 2) + d)
    return jnp.asarray(idx, dtype=I32)


def _prep_inproj_weights(w_in):
    o = [0]
    for s in (W_A, KVH_A * DH_A, KVH_A * DH_A, H_IDX * D_IDX, D_IDX, H_IDX,
              H_B * DK_B, H_B * DK_B, W_B, W_B):
        o.append(o[-1] + s)
    q_a, k_a, v_a, q_i, k_i, w_i, q_b, k_b, v_b, g_b = [w_in[:, o[i]:o[i + 1]] for i in range(10)]
    perm = _retention_perm()
    pad = jnp.zeros((D_MODEL, 64), w_in.dtype)
    wstd = jnp.concatenate([k_a, v_a, k_i, pad, q_b[:, perm], k_b[:, perm], v_b, g_b], axis=1)
    wt = jnp.concatenate([q_a, q_i, v_a, w_i], axis=1).T
    return wstd.astype(BF16), wt.astype(BF16)


def _inproj(x2d, ln_g, wstd, wt, tm):
    n = x2d.shape[0]
    assert n % tm == 0
    row = lambda c: pl.BlockSpec((tm, c), lambda i: (i, 0))
    col = lambda r: pl.BlockSpec((r, tm), lambda i: (0, i))
    full = lambda a: pl.BlockSpec(a.shape, lambda i: (0,) * a.ndim)
    g2 = ln_g.reshape(1, D_MODEL)
    out_shape = (
        jax.ShapeDtypeStruct((n, 128), F32), jax.ShapeDtypeStruct((n, 128), F32),
        jax.ShapeDtypeStruct((n, 64), F32),
        jax.ShapeDtypeStruct((n, 256), F32), jax.ShapeDtypeStruct((n, 256), F32),
        jax.ShapeDtypeStruct((n, 512), F32), jax.ShapeDtypeStruct((n, 512), F32),
        jax.ShapeDtypeStruct((512, n), F32), jax.ShapeDtypeStruct((512, n), F32),
        jax.ShapeDtypeStruct((128, n), F32), jax.ShapeDtypeStruct((8, n), F32),
    )
    out_specs = (row(128), row(128), row(64), row(256), row(256), row(512), row(512),
                 col(512), col(512), col(128), col(8))
    return pl.pallas_call(
        _inproj_kernel,
        grid=(n // tm,),
        in_specs=[row(D_MODEL), full(g2), full(wstd), full(wt)],
        out_specs=out_specs,
        out_shape=out_shape,
        compiler_params=_params("parallel"),
        name="inproj",
    )(x2d, g2, wstd, wt)


_KEY_ROWS = 256


def _dsa_body(qaT_ref, qiT_ref, wiT_ref, ka_ref, ki_ref, vaT_ref, oT_ref, score_ref, bias_ref,
              qpad_ref, s_ref, p_ref, m_ref, l_ref,
              *, n_keys, tq, qpos0, n_valid, topk):
    ch = _KEY_ROWS
    nch = n_keys // ch
    row_iota = lax.broadcasted_iota(I32, (ch, tq), 0)
    lane_iota = lax.broadcasted_iota(I32, (ch, tq), 1)
    q_chunk = (qpos0 + lane_iota) >> _CHUNK_SHIFT
    wi = wiT_ref[...]

    def allowed(srow):
        return ((srow >> _CHUNK_SHIFT) <= q_chunk) & (srow < n_valid)

    qi = [qiT_ref[D_IDX * h:D_IDX * (h + 1), :].astype(BF16) for h in range(H_IDX)]

    def score_chunk(c, carry):
        r0 = pl.multiple_of(c * ch, ch)
        kic = ki_ref[pl.ds(r0, ch), :].astype(BF16)
        acc = jnp.zeros((ch, tq), F32)
        for h in range(H_IDX):
            d = jnp.dot(kic, qi[h], preferred_element_type=F32)
            acc = acc + wi[h:h + 1, :] * jnp.maximum(d, 0.0)
        score_ref[pl.ds(r0, ch), :] = jnp.where(allowed(r0 + row_iota), acc, MASKED_SCORE)
        return carry

    lax.fori_loop(0, nch, score_chunk, 0)

    def count(pred):
        def body(c, acc):
            r0 = pl.multiple_of(c * ch, ch)
            x = score_ref[pl.ds(r0, ch), :]
            hit = jnp.where(pred(x, r0 + row_iota), 1.0, 0.0)
            return acc + jnp.sum(hit.reshape(ch // 8, 8, tq), axis=0)
        acc = lax.fori_loop(0, nch, body, jnp.zeros((8, tq), F32), unroll=min(nch, 4))
        return jnp.sum(acc, axis=0, keepdims=True)

    t = _kth_largest_key(lambda piv: count(lambda x, r: x >= piv), topk)
    lo = _key_float(t)
    hi = _key_float(t + 1)
    need = float(topk) - count(lambda x, r: x >= hi)

    tri = jnp.where(lax.broadcasted_iota(I32, (ch, ch), 0) > lax.broadcasted_iota(I32, (ch, ch), 1),
                    1.0, 0.0).astype(BF16)

    def bias_chunk(c, ties_before):
        r0 = pl.multiple_of(c * ch, ch)
        x = score_ref[pl.ds(r0, ch), :]
        tie = (x >= lo) & (x < hi)
        tie_f = jnp.where(tie, 1.0, 0.0)
        before = ties_before + jnp.dot(tri, tie_f.astype(BF16), preferred_element_type=F32)
        sel = ((x >= hi) | (tie & (before < need))) & allowed(r0 + row_iota)
        bias_ref[pl.ds(r0, ch), :] = jnp.where(sel, 0.0, NEG_BIG)
        return ties_before + jnp.sum(tie_f, axis=0, keepdims=True)

    lax.fori_loop(0, nch, bias_chunk, jnp.zeros((1, tq), F32))

    zeros_half = jnp.zeros((DH_A, tq), F32)
    scale = DH_A ** -0.5
    for h in range(H_A):
        qh = qaT_ref[DH_A * h:DH_A * (h + 1), :] * scale
        qpad = (jnp.concatenate([qh, zeros_half], axis=0) if h // GRP_A == 0
                else jnp.concatenate([zeros_half, qh], axis=0))
        qpad_ref[:, tq * h:tq * (h + 1)] = qpad.astype(BF16)
    m_ref[...] = jnp.full(m_ref.shape, NEG_BIG, F32)
    l_ref[...] = jnp.zeros(l_ref.shape, F32)
    oT_ref[...] = jnp.zeros(oT_ref.shape, F32)

    def att_chunk(c, carry):
        r0 = pl.multiple_of(c * ch, ch)
        kc = ka_ref[pl.ds(r0, ch), :].astype(BF16)
        bias = bias_ref[pl.ds(r0, ch), :]
        s_ref[...] = jnp.dot(kc, qpad_ref[...], preferred_element_type=F32)
        alphas = []
        for h in range(H_A):
            lanes = slice(tq * h, tq * (h + 1))
            s = s_ref[:, lanes] + bias
            m_old = m_ref[h:h + 1, :]
            m_new = jnp.maximum(m_old, jnp.max(s, axis=0, keepdims=True))
            alpha = jnp.exp(m_old - m_new)
            p = jnp.exp(s - m_new)
            l_ref[h:h + 1, :] = l_ref[h:h + 1, :] * alpha + jnp.sum(p, axis=0, keepdims=True)
            p_ref[:, lanes] = p.astype(BF16)
            m_ref[h:h + 1, :] = m_new
            alphas.append(alpha)
        for g in range(KVH_A):
            vc = vaT_ref[DH_A * g:DH_A * (g + 1), pl.ds(r0, ch)].astype(BF16)
            pv = jnp.dot(vc, p_ref[:, GRP_A * tq * g:GRP_A * tq * (g + 1)],
                         preferred_element_type=F32)
            for hh in range(GRP_A):
                h = GRP_A * g + hh
                rows = slice(DH_A * h, DH_A * (h + 1))
                oT_ref[rows, :] = oT_ref[rows, :] * alphas[h] + pv[:, tq * hh:tq * (hh + 1)]
        return carry

    lax.fori_loop(0, nch, att_chunk, 0)
    for h in range(H_A):
        rows = slice(DH_A * h, DH_A * (h + 1))
        oT_ref[rows, :] = oT_ref[rows, :] / l_ref[h:h + 1, :]


def _dsa_kernel(qaT_ref, qiT_ref, wiT_ref, ka_ref, ki_ref, vaT_ref, oT_ref, score_ref, bias_ref,
              qpad_ref, s_ref, p_ref, m_ref, l_ref,
                *, tq, variants, topk):
    j = pl.program_id(1)
    for jj, (n_keys, qpos0, n_valid) in enumerate(variants):
        @pl.when(j == jj)
        def _(n_keys=n_keys, qpos0=qpos0, n_valid=n_valid):
            _dsa_body(qaT_ref, qiT_ref, wiT_ref, ka_ref, ki_ref, vaT_ref, oT_ref, score_ref, bias_ref,
              qpad_ref, s_ref, p_ref, m_ref, l_ref,
                      n_keys=n_keys, tq=tq, qpos0=qpos0, n_valid=n_valid, topk=topk)


def _dsa(qaT, qiT, wiT, ka, ki, vaT, *, nb, tq, variants, topk):
    nj = len(variants)
    s_max = ka.shape[1]
    qspec = lambda r: pl.BlockSpec((r, tq), lambda b, j: (0, b * nj + j))
    return pl.pallas_call(
        functools.partial(_dsa_kernel, tq=tq, variants=variants, topk=topk),
        grid=(nb, nj),
        in_specs=[qspec(512), qspec(512), qspec(8),
                  pl.BlockSpec((None, s_max, 128), lambda b, j: (b, 0, 0)),
                  pl.BlockSpec((None, s_max, 64), lambda b, j: (b, 0, 0)),
                  pl.BlockSpec((128, s_max), lambda b, j: (0, b))],
        out_specs=qspec(512),
        out_shape=jax.ShapeDtypeStruct((512, nb * nj * tq), F32),
        scratch_shapes=[pltpu.VMEM((s_max, tq), F32), pltpu.VMEM((s_max, tq), F32),
                        pltpu.VMEM((KVH_A * DH_A, H_A * tq), BF16),
                        pltpu.VMEM((_KEY_ROWS, H_A * tq), F32),
                        pltpu.VMEM((_KEY_ROWS, H_A * tq), BF16),
                        pltpu.VMEM((H_A, tq), F32), pltpu.VMEM((H_A, tq), F32)],
        compiler_params=_params("parallel", "arbitrary"),
        name="dsa",
    )(qaT, qiT, wiT, ka, ki, vaT)


def _dsa_few_kernel(qa_ref, qi_ref, wi_ref, ka_ref, ki_ref, va_ref, kan_ref, kin_ref, van_ref, o_ref,
                    *, nq, n_keys, qpos0, n_valid, topk):
    def against_keys(q_ref, cached_ref, new_ref):
        q = q_ref[...].astype(BF16)
        return jnp.concatenate(
            [lax.dot_general(q, cached_ref[...].astype(BF16), _NT, preferred_element_type=F32),
             lax.dot_general(q, new_ref[...].astype(BF16), _NT, preferred_element_type=F32)], axis=1)

    dots = against_keys(qi_ref, ki_ref, kin_ref)
    wi = wi_ref[...]
    score = jnp.zeros((nq, n_keys), F32)
    for h in range(H_IDX):
        score = score + wi[:, h:h + 1] * jnp.maximum(dots[nq * h:nq * (h + 1), :], 0.0)
    key_pos = lax.broadcasted_iota(I32, (nq, n_keys), 1)
    q_pos = qpos0 + lax.broadcasted_iota(I32, (nq, n_keys), 0)
    allowed = ((key_pos >> _CHUNK_SHIFT) <= (q_pos >> _CHUNK_SHIFT)) & (key_pos < n_valid)
    x = jnp.where(allowed, score, MASKED_SCORE)

    def count(pred):
        return jnp.sum(jnp.where(pred, 1.0, 0.0), axis=1, keepdims=True)

    t = _kth_largest_key(lambda piv: count(x >= piv), topk)
    lo = _key_float(t)
    hi = _key_float(t + 1)
    need = float(topk) - count(x >= hi)
    tie = (x >= lo) & (x < hi)
    tie_f = jnp.where(tie, 1.0, 0.0)
    tri = jnp.where(lax.broadcasted_iota(I32, (LANE, LANE), 0) < lax.broadcasted_iota(I32, (LANE, LANE), 1),
                    1.0, 0.0).astype(BF16)
    running = jnp.zeros((nq, 1), F32)
    before = []
    for c in range(n_keys // LANE):
        blk = tie_f[:, LANE * c:LANE * (c + 1)]
        before.append(running + jnp.dot(blk.astype(BF16), tri, preferred_element_type=F32))
        running = running + jnp.sum(blk, axis=1, keepdims=True)
    before = jnp.concatenate(before, axis=1)
    sel = ((x >= hi) | (tie & (before < need))) & allowed
    bias = jnp.where(sel, 0.0, NEG_BIG)

    s = against_keys(qa_ref, ka_ref, kan_ref) + jnp.concatenate([bias] * H_A, axis=0)
    m = jnp.max(s, axis=1, keepdims=True)
    p = jnp.exp(s - m)
    l = jnp.sum(p, axis=1, keepdims=True)
    pb = p.astype(BF16)
    n_cached = n_keys - LANE
    pv = (jnp.dot(pb[:, :n_cached], va_ref[...].astype(BF16), preferred_element_type=F32)
          + jnp.dot(pb[:, n_cached:], van_ref[...].astype(BF16), preferred_element_type=F32)) / l
    for h in range(H_A):
        g = h // GRP_A
        o_ref[:, DH_A * h:DH_A * (h + 1)] = pv[nq * h:nq * (h + 1), DH_A * g:DH_A * (g + 1)]


def _dsa_few(qa_rows, qi_rows, wi, ka, ki, va, ka_new, ki_new, va_new, *, nq, qpos0, n_valid, topk):
    nb, past = ka.shape[0], ka.shape[1]
    assert past % LANE == 0 and ka_new.shape[1] == LANE
    n_keys = past + LANE
    blk = lambda r, c: pl.BlockSpec((None, r, c), lambda b: (b, 0, 0))
    return pl.pallas_call(
        functools.partial(_dsa_few_kernel, nq=nq, n_keys=n_keys, qpos0=qpos0, n_valid=n_valid, topk=topk),
        grid=(nb,),
        in_specs=[blk(H_A * nq, 128), blk(H_IDX * nq, 64), blk(nq, H_IDX),
                  blk(past, 128), blk(past, 64), blk(past, 128),
                  blk(LANE, 128), blk(LANE, 64), blk(LANE, 128)],
        out_specs=pl.BlockSpec((nq, W_A), lambda b: (b, 0)),
        out_shape=jax.ShapeDtypeStruct((nb * nq, W_A), F32),
        compiler_params=_params("parallel"),
        name="dsa_few",
    )(qa_rows, qi_rows, wi, ka, ki, va, ka_new, ki_new, va_new)


def _ret_kernel(q_ref, k_ref, v_ref, g_ref, cos_ref, sin_ref, dmask_ref, rdec_ref, kdec_ref,
                sdec_ref, s0_ref, o_ref, sout_ref, state_ref, *, c_len, nsub):
    ci = pl.program_id(1)

    @pl.when(ci == 0)
    def _():
        state_ref[...] = s0_ref[...]

    lane_head = (lax.broadcasted_iota(I32, (1, 2 * LANE), 1) % LANE) // (DK_B // 2)
    row_head = (lax.broadcasted_iota(I32, (2 * LANE, DV_B), 0) % LANE) // (DK_B // 2)
    for sub in range(nsub):
        rows = slice(sub * c_len, (sub + 1) * c_len)
        q = q_ref[rows, :]
        k = k_ref[rows, :]
        v = v_ref[rows, :]
        g = g_ref[rows, :]
        cos = cos_ref[rows, :]
        sin = sin_ref[rows, :]
        q1, q2 = q[:, :LANE], q[:, LANE:]
        k1, k2 = k[:, :LANE], k[:, LANE:]
        qr = jnp.concatenate([q1 * cos - q2 * sin, q1 * sin + q2 * cos], axis=1)
        kr = jnp.concatenate([k1 * cos - k2 * sin, k1 * sin + k2 * cos], axis=1) * (DK_B ** -0.5)
        state = state_ref[...]
        upd = _bdot(kr * kdec_ref[...], v, _TN)
        new_state = sdec_ref[...] * state
        vb = v.astype(BF16)
        krb = kr.astype(BF16)
        sb = state.astype(BF16)
        for h in range(H_B):
            qm = jnp.where(lane_head == h, qr, 0.0).astype(BF16)
            qk = lax.dot_general(qm, krb, _NT, preferred_element_type=F32) * dmask_ref[h]
            intra = jnp.dot(qk.astype(BF16), vb[:, DV_B * h:DV_B * (h + 1)], preferred_element_type=F32)
            inter = jnp.dot(qm, sb, preferred_element_type=F32) * rdec_ref[h]
            o = intra + inter
            mu = jnp.mean(o, axis=-1, keepdims=True)
            d = o - mu
            var = jnp.mean(d * d, axis=-1, keepdims=True)
            gh = g[:, DV_B * h:DV_B * (h + 1)]
            silu = gh * (1.0 / (1.0 + jnp.exp(-gh)))
            o_ref[rows, DV_B * h:DV_B * (h + 1)] = silu * (d * lax.rsqrt(var + EPS))
            new_state = new_state + jnp.where(row_head == h, upd[:, DV_B * h:DV_B * (h + 1)], 0.0)
        state_ref[...] = new_state

    @pl.when(ci == pl.num_programs(1) - 1)
    def _():
        sout_ref[...] = state_ref[...]


def _retention_tables(pos, c_len):
    half = DK_B // 2
    lg = jnp.log(1.0 - 2.0 ** (-5.0 - jnp.arange(H_B, dtype=F32)))
    inv = 1.0 / (ROT_BASE ** jnp.linspace(0.0, 1.0, half, dtype=F32))
    ang = pos.astype(F32)[:, None] * inv[None, :]
    cos = jnp.tile(jnp.cos(ang), (1, H_B))
    sin = jnp.tile(jnp.sin(ang), (1, H_B))
    i = jnp.arange(c_len, dtype=F32)
    diff = i[:, None] - i[None, :]
    dmask = jnp.where(diff >= 0, jnp.exp(jnp.maximum(diff, 0.0)[None] * lg[:, None, None]), 0.0)
    rdec = jnp.broadcast_to(jnp.exp((i + 1.0)[None, :] * lg[:, None])[:, :, None], (H_B, c_len, DV_B))
    lane_head = (jnp.arange(2 * LANE) % LANE) // half
    kdec = jnp.exp((c_len - 1.0 - i)[:, None] * lg[lane_head][None, :])
    sdec = jnp.broadcast_to(jnp.exp(c_len * lg[lane_head])[:, None], (2 * LANE, DV_B))
    return cos, sin, dmask.astype(F32), rdec.astype(F32), kdec.astype(F32), sdec.astype(F32)


def _state_to_packed(s):
    b = s.shape[0]
    return s.reshape(b, H_B, 2, DK_B // 2, DV_B).transpose(0, 2, 1, 3, 4).reshape(b, 2 * LANE, DV_B)


def _state_from_packed(s):
    b = s.shape[0]
    return s.reshape(b, 2, H_B, DK_B // 2, DV_B).transpose(0, 2, 1, 3, 4).reshape(b, H_B, DK_B, DV_B)


def _retention(qb, kb, vb, gb, s0_packed, pos, *, nb, t_len, c_len, nsub):
    blk = c_len * nsub
    nc = t_len // blk
    cos, sin, dmask, rdec, kdec, sdec = _retention_tables(pos, c_len)
    tok = lambda c: pl.BlockSpec((blk, c), lambda b, i: (b * nc + i, 0))
    tab = pl.BlockSpec((blk, LANE), lambda b, i: (i, 0))
    full = lambda a: pl.BlockSpec(a.shape, lambda b, i: (0,) * a.ndim)
    st = pl.BlockSpec((None, 2 * LANE, DV_B), lambda b, i: (b, 0, 0))
    return pl.pallas_call(
        functools.partial(_ret_kernel, c_len=c_len, nsub=nsub),
        grid=(nb, nc),
        in_specs=[tok(256), tok(256), tok(512), tok(512), tab, tab,
                  full(dmask), full(rdec), full(kdec), full(sdec), st],
        out_specs=(tok(512), st),
        out_shape=(jax.ShapeDtypeStruct((nb * t_len, W_B), F32),
                   jax.ShapeDtypeStruct((nb, 2 * LANE, DV_B), F32)),
        scratch_shapes=[pltpu.VMEM((2 * LANE, DV_B), F32)],
        compiler_params=_params("parallel", "arbitrary"),
        name="retention",
    )(qb, kb, vb, gb, cos, sin, dmask, rdec, kdec, sdec, s0_packed)


def _outproj_kernel(x_ref, oaT_ref, ob_ref, woa_ref, wob_ref, g2_ref, wq_ref, k1_ref, k2_ref,
                    x1_ref, h2T_ref, s1T_ref, s2T_ref):
    mix = _bdot(oaT_ref[...], woa_ref[...], _TN) + _bdot(ob_ref[...], wob_ref[...])
    x1 = x_ref[...] + mix
    x1_ref[...] = x1
    h2 = x1 * lax.rsqrt(jnp.mean(x1 * x1, axis=-1, keepdims=True) + EPS) * g2_ref[...]
    h2T_ref[...] = h2.T.astype(BF16)
    q = _bdot(h2, wq_ref[...])
    k1 = k1_ref[...]
    k2 = k2_ref[...]
    for h in range(PEER_HEADS):
        base = 2 * D_PHALF * h
        s1T_ref[h] = _bdot(k1, q[:, base:base + D_PHALF], _NT) * LOG2E
        s2T_ref[h] = _bdot(k2, q[:, base + D_PHALF:base + 2 * D_PHALF], _NT) * LOG2E


def _outproj(x2d, oaT, ob, woa, wob, ln2_g, wq, k1, k2, tm):
    n = x2d.shape[0]
    assert n % tm == 0
    g2 = ln2_g.reshape(1, D_MODEL)
    row = lambda c: pl.BlockSpec((tm, c), lambda i: (i, 0))
    col = lambda r: pl.BlockSpec((r, tm), lambda i: (0, i))
    full = lambda a: pl.BlockSpec(a.shape, lambda i: (0,) * a.ndim)
    sc = pl.BlockSpec((PEER_HEADS, N_KEYS, tm), lambda i: (0, 0, i))
    return pl.pallas_call(
        _outproj_kernel,
        grid=(n // tm,),
        in_specs=[row(D_MODEL), col(W_A), row(W_B), full(woa), full(wob), full(g2), full(wq),
                  full(k1), full(k2)],
        out_specs=(row(D_MODEL), col(D_MODEL), sc, sc),
        out_shape=(jax.ShapeDtypeStruct((n, D_MODEL), F32),
                   jax.ShapeDtypeStruct((D_MODEL, n), BF16),
                   jax.ShapeDtypeStruct((PEER_HEADS, N_KEYS, n), F32),
                   jax.ShapeDtypeStruct((PEER_HEADS, N_KEYS, n), F32)),
        compiler_params=_params("parallel"),
        name="outproj",
    )(x2d, oaT, ob, woa, wob, g2, wq, k1, k2)


def _peer_sel_kernel(s1T_ref, s2T_ref, thr_ref, s1s_ref, a_ref, b_ref, cand_ref):
    def top16(s, dst_ref):
        for r in range(PEER_TOPK):
            m = jnp.max(s, axis=0, keepdims=True)
            dst_ref[r:r + 1, :] = m
            s = jnp.where(s == m, -jnp.inf, s)

    def pair_sums(a, b):
        sums = [a + b[0:1]] + [a[0:8] + b[j:j + 1] for j in range(1, 8)] + [a[0:1] + b[8:16]]
        return jnp.concatenate(sums, axis=0)

    def candidates(h, carry):
        top16(s1T_ref[h], a_ref.at[h])
        top16(s2T_ref[h], b_ref.at[h])
        cand_ref[h] = pair_sums(a_ref[h], b_ref[h])
        return carry

    lax.fori_loop(0, PEER_HEADS, candidates, 0)

    def count_ge(piv):
        rows = []
        for h in range(PEER_HEADS):
            p = piv if jnp.ndim(piv) == 0 else piv[h:h + 1, :]
            rows.append(jnp.sum(jnp.where(cand_ref[h] >= p, 1.0, 0.0), axis=0, keepdims=True))
        return jnp.concatenate(rows, axis=0)

    thr_all = _key_float(_kth_largest_key(count_ge, PEER_TOPK))
    for h in range(PEER_HEADS):
        cand = cand_ref[h]
        chosen = cand >= thr_all[h:h + 1, :]
        top = cand[0:1]
        z = jnp.sum(jnp.where(chosen, jnp.exp2(cand - top), 0.0), axis=0, keepdims=True)
        shift = top + jnp.log2(z)
        shifted = pair_sums(a_ref[h] - shift, b_ref[h])
        thr_ref[h] = jnp.min(jnp.where(chosen, shifted, jnp.inf), axis=0, keepdims=True)
        s1s_ref[h] = s1T_ref[h] - shift


def _peer_select(s1T, s2T, tl):
    n = s1T.shape[2]
    assert n % tl == 0
    sc = pl.BlockSpec((PEER_HEADS, N_KEYS, tl), lambda i: (0, 0, i))
    st = pl.BlockSpec((PEER_HEADS, 1, tl), lambda i: (0, 0, i))
    stat = jax.ShapeDtypeStruct((PEER_HEADS, 1, n), F32)
    return pl.pallas_call(
        _peer_sel_kernel,
        grid=(n // tl,),
        in_specs=[sc, sc],
        out_specs=(st, sc),
        out_shape=(stat, jax.ShapeDtypeStruct((PEER_HEADS, N_KEYS, n), F32)),
        scratch_shapes=[pltpu.VMEM((PEER_HEADS, PEER_TOPK, tl), F32),
                        pltpu.VMEM((PEER_HEADS, PEER_TOPK, tl), F32),
                        pltpu.VMEM((PEER_HEADS, 80, tl), F32)],
        compiler_params=_params("parallel"),
        name="peer_select",
    )(s1T, s2T)


_I1_PER_BLOCK = 16
_EXPERT_BLOCK = _I1_PER_BLOCK * N_KEYS
_MM_ROWS = 256
_SQRT_HALF = 0.7071067811865476


def _peer_dense_kernel(h2T_ref, u_ref, vT_ref, s1s_ref, s2T_ref, thr_ref,
                       x1_ref, gf_ref, y_ref, aT_ref, wT_ref, acc_ref, *, tm):
    j = pl.program_id(1)
    n_i1 = _I1_PER_BLOCK
    n_lt = tm // LANE
    n_half = 2 if n_lt % 2 == 0 else 1
    lt_per_half = n_lt // n_half
    half_lanes = [slice(hf * lt_per_half * LANE, (hf + 1) * lt_per_half * LANE) for hf in range(n_half)]

    @pl.when(j == 0)
    def _():
        acc_ref[...] = jnp.zeros_like(acc_ref)

    i1_base = pl.multiple_of(j * n_i1, n_i1)

    def act_piece(hf, p):
        rows = slice(_MM_ROWS * p, _MM_ROWS * (p + 1))
        aT_ref[rows, half_lanes[hf]] = jnp.dot(u_ref[rows, :], h2T_ref[:, half_lanes[hf]],
                                               preferred_element_type=F32)

    def out_piece(hf, r):
        rows = slice(_MM_ROWS * r, _MM_ROWS * (r + 1))
        acc_ref[rows, half_lanes[hf]] += jnp.dot(vT_ref[rows, :], wT_ref[:, half_lanes[hf]],
                                                 preferred_element_type=F32)

    def gate_group(hf, ii):
        rows = slice(N_KEYS * ii, N_KEYS * (ii + 1))
        for lt in range(lt_per_half):
            lanes = slice((hf * lt_per_half + lt) * LANE, (hf * lt_per_half + lt + 1) * LANE)
            gate = jnp.zeros((N_KEYS, LANE), F32)
            for h in range(PEER_HEADS):
                s1 = s1s_ref[h, pl.ds(i1_base, n_i1), lanes][ii:ii + 1]
                sm = s1 + s2T_ref[h, :, lanes]
                gate = gate + jnp.where(sm >= thr_ref[h, :, lanes], jnp.exp2(sm), 0.0)
            a = aT_ref[rows, lanes]
            act = 0.5 * a * (1.0 + lax.erf(a * _SQRT_HALF))
            wT_ref[rows, lanes] = (gate * act).astype(BF16)

    pending = [functools.partial(act_piece, hf, p) for hf in range(n_half)
               for p in range(_EXPERT_BLOCK // _MM_ROWS)]
    pending.pop(0)()
    pending.pop(0)()
    for hf in range(n_half):
        for ii in range(n_i1):
            if pending and (hf, ii) != (0, 0):
                pending.pop(0)()
            gate_group(hf, ii)
        pending.extend(functools.partial(out_piece, hf, r) for r in range(D_MODEL // _MM_ROWS))
    for piece in pending:
        piece()

    @pl.when(j == pl.num_programs(1) - 1)
    def _():
        x2 = x1_ref[...] + acc_ref[...].T
        y_ref[...] = x2 * lax.rsqrt(jnp.mean(x2 * x2, axis=-1, keepdims=True) + EPS) * gf_ref[...]


def _peer_dense(h2T, u_bf, vT_bf, s1s, s2T, thr, x1, lnf_g, tm):
    n = x1.shape[0]
    assert n % tm == 0
    ne = u_bf.shape[0] // _EXPERT_BLOCK
    gf = lnf_g.reshape(1, D_MODEL)
    sc = pl.BlockSpec((PEER_HEADS, N_KEYS, tm), lambda i, j: (0, 0, i))
    st = pl.BlockSpec((PEER_HEADS, 1, tm), lambda i, j: (0, 0, i))
    return pl.pallas_call(
        functools.partial(_peer_dense_kernel, tm=tm),
        grid=(n // tm, ne),
        in_specs=[pl.BlockSpec((D_MODEL, tm), lambda i, j: (0, i)),
                  pl.BlockSpec((_EXPERT_BLOCK, D_MODEL), lambda i, j: (j, 0)),
                  pl.BlockSpec((D_MODEL, _EXPERT_BLOCK), lambda i, j: (0, j)),
                  sc, sc, st,
                  pl.BlockSpec((tm, D_MODEL), lambda i, j: (i, 0)),
                  pl.BlockSpec((1, D_MODEL), lambda i, j: (0, 0))],
        out_specs=pl.BlockSpec((tm, D_MODEL), lambda i, j: (i, 0)),
        out_shape=jax.ShapeDtypeStruct((n, D_MODEL), F32),
        scratch_shapes=[pltpu.VMEM((_EXPERT_BLOCK, tm), F32),
                        pltpu.VMEM((_EXPERT_BLOCK, tm), BF16),
                        pltpu.VMEM((D_MODEL, tm), F32)],
        compiler_params=_params("parallel", "arbitrary"),
        name="peer_dense",
    )(h2T, u_bf, vT_bf, s1s, s2T, thr, x1, gf)


def _mix_to_output(x2d, oaT, ob, lw, tm_out, tl_sel, tm_dense):
    x1, h2T, s1T, s2T = _outproj(x2d, oaT, ob, lw["woa"], lw["wob"], lw["ln2_g"], lw["wq"],
                                 lw["k1"], lw["k2"], tm_out)
    thr, s1s = _peer_select(s1T, s2T, tl_sel)
    return _peer_dense(h2T, lw["u"], lw["vT"], s1s, s2T, thr, x1, lw["lnf_g"], tm_dense)


def _layer_prompt(x, lw):
    nb, t_len, _ = x.shape
    n = nb * t_len
    x2d = x.reshape(n, D_MODEL)
    ka, va, ki, qb, kb, vb, gb, qaT, qiT, vaT, wiT = _inproj(x2d, lw["ln1_g"], lw["wstd"], lw["wt"], 512)
    tq = 256
    nj = t_len // tq
    variants = tuple((tq * (j + 1), tq * j, tq * (j + 1)) for j in range(nj))
    oaT = _dsa(qaT, qiT, wiT, ka.reshape(nb, t_len, 128), ki.reshape(nb, t_len, 64), vaT,
               nb=nb, tq=tq, variants=variants, topk=min(TOPK_MAX, t_len // 4))
    s0 = jnp.zeros((nb, 2 * LANE, DV_B), F32)
    ob, s_fin = _retention(qb, kb, vb, gb, s0, jnp.arange(t_len, dtype=I32),
                           nb=nb, t_len=t_len, c_len=CHUNK, nsub=4)
    y = _mix_to_output(x2d, oaT, ob, lw, 512, 256, 512)
    return (y.reshape(nb, t_len, D_MODEL), ka.reshape(nb, t_len, KVH_A, DH_A),
            va.reshape(nb, t_len, KVH_A, DH_A), ki.reshape(nb, t_len, D_IDX), _state_from_packed(s_fin))


def _layer_sample(x, ck, cv, cki, cs, lw):
    nb, t_len, _ = x.shape
    past = ck.shape[1]
    n = nb * t_len
    x2d = x.reshape(n, D_MODEL)
    ka, va, ki, qb, kb, vb, gb, qaT, qiT, vaT, wiT = _inproj(x2d, lw["ln1_g"], lw["wstd"], lw["wt"], n)
    total = past + t_len
    new_block = lambda a: jnp.pad(a.reshape(nb, t_len, -1), ((0, 0), (0, LANE - t_len), (0, 0)))
    per_head = lambda a, d: a.reshape(H_A, d, nb, t_len).transpose(2, 0, 3, 1)
    qa = per_head(qaT, DH_A) * (DH_A ** -0.5)
    group_cols = (jnp.arange(KVH_A * DH_A) // DH_A)[None, :] == (jnp.arange(H_A) // GRP_A)[:, None]
    qa_rows = jnp.where(group_cols[None, :, None, :], jnp.concatenate([qa] * KVH_A, axis=-1), 0.0)
    oa = _dsa_few(qa_rows.reshape(nb, H_A * t_len, KVH_A * DH_A),
                  per_head(qiT, D_IDX).reshape(nb, H_IDX * t_len, D_IDX),
                  wiT.reshape(H_IDX, nb, t_len).transpose(1, 2, 0),
                  ck.reshape(nb, past, KVH_A * DH_A), cki, cv.reshape(nb, past, KVH_A * DH_A),
                  new_block(ka), new_block(ki), new_block(va),
                  nq=t_len, qpos0=past, n_valid=total, topk=min(TOPK_MAX, total // 4))
    oaT = oa.T
    ob, s_new = _retention(qb, kb, vb, gb, _state_to_packed(cs), past + jnp.arange(t_len, dtype=I32),
                           nb=nb, t_len=t_len, c_len=t_len, nsub=1)
    y = _mix_to_output(x2d, oaT, ob, lw, n, n, n)
    return (y.reshape(nb, t_len, D_MODEL), ka.reshape(nb, t_len, KVH_A, DH_A),
            va.reshape(nb, t_len, KVH_A, DH_A), ki.reshape(nb, t_len, D_IDX), _state_from_packed(s_new))


def kernel(x_prompt, x_sample, cache_attn_k, cache_attn_v, cache_idx_k, state_retention,
           ln1_g, w_in, w_out, ln2_g, peer_w_query, peer_keys1, peer_keys2, peer_u, peer_v, ln_final_g):
    depth = w_in.shape[0]
    assert depth == 1, "the final norm is fused into the (single) layer's last kernel"
    wstd, wt = _prep_inproj_weights(w_in[0])
    lw = dict(ln1_g=ln1_g[0], wstd=wstd, wt=wt,
              woa=w_out[0, :W_A].astype(BF16), wob=w_out[0, W_A:].astype(BF16),
              ln2_g=ln2_g[0], wq=peer_w_query[0].astype(BF16),
              k1=peer_keys1[0], k2=peer_keys2[0],
              u=peer_u[0].astype(BF16), vT=peer_v[0].T.astype(BF16), lnf_g=ln_final_g)
    yp, kp, vp, kip, sp = _layer_prompt(x_prompt, lw)
    ys, ks, vs, kis, ss = _layer_sample(x_sample, cache_attn_k[0], cache_attn_v[0], cache_idx_k[0],
                                        state_retention[0], lw)
    st = lambda a: a[None]
    return (yp, ys, st(kp), st(vp), st(kip), st(sp), st(ks), st(vs), st(kis), st(ss))
```

```python
import functools

import jax
import jax.numpy as jnp
from jax import lax
from jax.experimental import pallas as pl
from jax.experimental.pallas import tpu as pltpu

F32 = jnp.float32
BF16 = jnp.bfloat16
I32 = jnp.int32

D_MODEL = 1024
CHUNK = 64
_CHUNK_SHIFT = 6
assert 1 << _CHUNK_SHIFT == CHUNK
EPS = 1e-6
H_A, KVH_A, DH_A = 8, 2, 64
GRP_A = H_A // KVH_A
H_IDX, D_IDX = 8, 64
TOPK_MAX = 256
H_B, DK_B, DV_B = 4, 64, 128
ROT_BASE = 10000.0
W_A = H_A * DH_A
W_B = H_B * DV_B
N_KEYS = 128
PEER_HEADS = 8
PEER_TOPK = 16
D_PHALF = 128

LOG2E = 1.4426950408889634
INT_MIN = -(2 ** 31)
NEG_BIG = -1e30
MASKED_SCORE = -3.0e38
LANE = 128
VMEM_LIMIT_BYTES = 56 * 1024 * 1024

_NT = (((1,), (1,)), ((), ()))
_TN = (((0,), (0,)), ((), ()))


def _bdot(a, b, dims=None):
    a = a.astype(BF16)
    b = b.astype(BF16)
    if dims is None:
        return jnp.dot(a, b, preferred_element_type=F32)
    return lax.dot_general(a, b, dims, preferred_element_type=F32)


def _params(*sem):
    return pltpu.CompilerParams(dimension_semantics=sem, vmem_limit_bytes=VMEM_LIMIT_BYTES)


def _key_float(k):
    return pltpu.bitcast(k ^ ((k >> 31) & 0x7FFFFFFF), F32)


def _kth_largest_key(count_ge, k):
    kf = float(k)
    t0 = jnp.where(count_ge(0.0) >= kf, 0, INT_MIN).astype(I32)

    def bit_body(i, t):
        cand = t | lax.shift_left(jnp.int32(1), 30 - i)
        return jnp.where(count_ge(_key_float(cand)) >= kf, cand, t)

    return lax.fori_loop(0, 31, bit_body, t0)


def _inproj_kernel(x_ref, g_ref, wstd_ref, wt_ref,
                   ka_ref, va_ref, ki_ref, qb_ref, kb_ref, vb_ref, gb_ref,
                   qaT_ref, qiT_ref, vaT_ref, wiT_ref):
    x = x_ref[...]
    h = x * lax.rsqrt(jnp.mean(x * x, axis=-1, keepdims=True) + EPS) * g_ref[...]
    hb = h.astype(BF16)
    std = jnp.dot(hb, wstd_ref[...], preferred_element_type=F32)
    ka_ref[...] = std[:, 0:128]
    va_ref[...] = std[:, 128:256]
    ki_ref[...] = std[:, 256:320]
    qb_ref[...] = std[:, 384:640]
    kb_ref[...] = std[:, 640:896]
    vb_ref[...] = std[:, 896:1408]
    gb_ref[...] = std[:, 1408:1920]
    t = lax.dot_general(wt_ref[...], hb, _NT, preferred_element_type=F32)
    qaT_ref[...] = t[0:512]
    qiT_ref[...] = t[512:1024]
    vaT_ref[...] = t[1024:1152]
    wiT_ref[...] = t[1152:1160]


def _retention_perm():
    idx = []
    for half in range(2):
        for h in range(H_B):
            for d in range(DK_B // 2):
                idx.append(h * DK_B + half * (DK_B // 2) + d)
    return jnp.asarray(idx, dtype=I32)


def _prep_inproj_weights(w_in):
    o = [0]
    for s in (W_A, KVH_A * DH_A, KVH_A * DH_A, H_IDX * D_IDX, D_IDX, H_IDX,
              H_B * DK_B, H_B * DK_B, W_B, W_B):
        o.append(o[-1] + s)
    q_a, k_a, v_a, q_i, k_i, w_i, q_b, k_b, v_b, g_b = [w_in[:, o[i]:o[i + 1]] for i in range(10)]
    perm = _retention_perm()
    pad = jnp.zeros((D_MODEL, 64), w_in.dtype)
    wstd = jnp.concatenate([k_a, v_a, k_i, pad, q_b[:, perm], k_b[:, perm], v_b, g_b], axis=1)
    wt = jnp.concatenate([q_a, q_i, v_a, w_i], axis=1).T
    return wstd.astype(BF16), wt.astype(BF16)


def _inproj(x2d, ln_g, wstd, wt, tm):
    n = x2d.shape[0]
    assert n % tm == 0
    row = lambda c: pl.BlockSpec((tm, c), lambda i: (i, 0))
    col = lambda r: pl.BlockSpec((r, tm), lambda i: (0, i))
    full = lambda a: pl.BlockSpec(a.shape, lambda i: (0,) * a.ndim)
    g2 = ln_g.reshape(1, D_MODEL)
    out_shape = (
        jax.ShapeDtypeStruct((n, 128), F32), jax.ShapeDtypeStruct((n, 128), F32),
        jax.ShapeDtypeStruct((n, 64), F32),
        jax.ShapeDtypeStruct((n, 256), F32), jax.ShapeDtypeStruct((n, 256), F32),
        jax.ShapeDtypeStruct((n, 512), F32), jax.ShapeDtypeStruct((n, 512), F32),
        jax.ShapeDtypeStruct((512, n), F32), jax.ShapeDtypeStruct((512, n), F32),
        jax.ShapeDtypeStruct((128, n), F32), jax.ShapeDtypeStruct((8, n), F32),
    )
    out_specs = (row(128), row(128), row(64), row(256), row(256), row(512), row(512),
                 col(512), col(512), col(128), col(8))
    return pl.pallas_call(
        _inproj_kernel,
        grid=(n // tm,),
        in_specs=[row(D_MODEL), full(g2), full(wstd), full(wt)],
        out_specs=out_specs,
        out_shape=out_shape,
        compiler_params=_params("parallel"),
        name="inproj",
    )(x2d, g2, wstd, wt)


_KEY_ROWS = 256


def _dsa_body(qaT_ref, qiT_ref, wiT_ref, ka_ref, ki_ref, vaT_ref, oT_ref, score_ref, bias_ref,
              qpad_ref, s_ref, p_ref, m_ref, l_ref,
              *, n_keys, tq, qpos0, n_valid, topk):
    ch = _KEY_ROWS
    nch = n_keys // ch
    row_iota = lax.broadcasted_iota(I32, (ch, tq), 0)
    lane_iota = lax.broadcasted_iota(I32, (ch, tq), 1)
    q_chunk = (qpos0 + lane_iota) >> _CHUNK_SHIFT
    wi = wiT_ref[...]

    def allowed(srow):
        return ((srow >> _CHUNK_SHIFT) <= q_chunk) & (srow < n_valid)

    qi = [qiT_ref[D_IDX * h:D_IDX * (h + 1), :].astype(BF16) for h in range(H_IDX)]

    def score_chunk(c, carry):
        r0 = pl.multiple_of(c * ch, ch)
        kic = ki_ref[pl.ds(r0, ch), :].astype(BF16)
        acc = jnp.zeros((ch, tq), F32)
        for h in range(H_IDX):
            d = jnp.dot(kic, qi[h], preferred_element_type=F32)
            acc = acc + wi[h:h + 1, :] * jnp.maximum(d, 0.0)
        score_ref[pl.ds(r0, ch), :] = jnp.where(allowed(r0 + row_iota), acc, MASKED_SCORE)
        return carry

    lax.fori_loop(0, nch, score_chunk, 0)

    def count(pred):
        def body(c, acc):
            r0 = pl.multiple_of(c * ch, ch)
            x = score_ref[pl.ds(r0, ch), :]
            hit = jnp.where(pred(x, r0 + row_iota), 1.0, 0.0)
            return acc + jnp.sum(hit.reshape(ch // 8, 8, tq), axis=0)
        acc = lax.fori_loop(0, nch, body, jnp.zeros((8, tq), F32), unroll=min(nch, 4))
        return jnp.sum(acc, axis=0, keepdims=True)

    t = _kth_largest_key(lambda piv: count(lambda x, r: x >= piv), topk)
    lo = _key_float(t)
    hi = _key_float(t + 1)
    need = float(topk) - count(lambda x, r: x >= hi)

    tri = jnp.where(lax.broadcasted_iota(I32, (ch, ch), 0) > lax.broadcasted_iota(I32, (ch, ch), 1),
                    1.0, 0.0).astype(BF16)

    def bias_chunk(c, ties_before):
        r0 = pl.multiple_of(c * ch, ch)
        x = score_ref[pl.ds(r0, ch), :]
        tie = (x >= lo) & (x < hi)
        tie_f = jnp.where(tie, 1.0, 0.0)
        before = ties_before + jnp.dot(tri, tie_f.astype(BF16), preferred_element_type=F32)
        sel = ((x >= hi) | (tie & (before < need))) & allowed(r0 + row_iota)
        bias_ref[pl.ds(r0, ch), :] = jnp.where(sel, 0.0, NEG_BIG)
        return ties_before + jnp.sum(tie_f, axis=0, keepdims=True)

    lax.fori_loop(0, nch, bias_chunk, jnp.zeros((1, tq), F32))

    zeros_half = jnp.zeros((DH_A, tq), F32)
    scale = DH_A ** -0.5
    for h in range(H_A):
        qh = qaT_ref[DH_A * h:DH_A * (h + 1), :] * scale
        qpad = (jnp.concatenate([qh, zeros_half], axis=0) if h // GRP_A == 0
                else jnp.concatenate([zeros_half, qh], axis=0))
        qpad_ref[:, tq * h:tq * (h + 1)] = qpad.astype(BF16)
    m_ref[...] = jnp.full(m_ref.shape, NEG_BIG, F32)
    l_ref[...] = jnp.zeros(l_ref.shape, F32)
    oT_ref[...] = jnp.zeros(oT_ref.shape, F32)

    def att_chunk(c, carry):
        r0 = pl.multiple_of(c * ch, ch)
        kc = ka_ref[pl.ds(r0, ch), :].astype(BF16)
        bias = bias_ref[pl.ds(r0, ch), :]
        s_ref[...] = jnp.dot(kc, qpad_ref[...], preferred_element_type=F32)
        alphas = []
        for h in range(H_A):
            lanes = slice(tq * h, tq * (h + 1))
            s = s_ref[:, lanes] + bias
            m_old = m_ref[h:h + 1, :]
            m_new = jnp.maximum(m_old, jnp.max(s, axis=0, keepdims=True))
            alpha = jnp.exp(m_old - m_new)
            p = jnp.exp(s - m_new)
            l_ref[h:h + 1, :] = l_ref[h:h + 1, :] * alpha + jnp.sum(p, axis=0, keepdims=True)
            p_ref[:, lanes] = p.astype(BF16)
            m_ref[h:h + 1, :] = m_new
            alphas.append(alpha)
        for g in range(KVH_A):
            vc = vaT_ref[DH_A * g:DH_A * (g + 1), pl.ds(r0, ch)].astype(BF16)
            pv = jnp.dot(vc, p_ref[:, GRP_A * tq * g:GRP_A * tq * (g + 1)],
                         preferred_element_type=F32)
            for hh in range(GRP_A):
                h = GRP_A * g + hh
                rows = slice(DH_A * h, DH_A * (h + 1))
                oT_ref[rows, :] = oT_ref[rows, :] * alphas[h] + pv[:, tq * hh:tq * (hh + 1)]
        return carry

    lax.fori_loop(0, nch, att_chunk, 0)
    for h in range(H_A):
        rows = slice(DH_A * h, DH_A * (h + 1))
        oT_ref[rows, :] = oT_ref[rows, :] / l_ref[h:h + 1, :]


def _dsa_kernel(qaT_ref, qiT_ref, wiT_ref, ka_ref, ki_ref, vaT_ref, oT_ref, score_ref, bias_ref,
              qpad_ref, s_ref, p_ref, m_ref, l_ref,
                *, tq, variants, topk):
    j = pl.program_id(1)
    for jj, (n_keys, qpos0, n_valid) in enumerate(variants):
        @pl.when(j == jj)
        def _(n_keys=n_keys, qpos0=qpos0, n_valid=n_valid):
            _dsa_body(qaT_ref, qiT_ref, wiT_ref, ka_ref, ki_ref, vaT_ref, oT_ref, score_ref, bias_ref,
              qpad_ref, s_ref, p_ref, m_ref, l_ref,
                      n_keys=n_keys, tq=tq, qpos0=qpos0, n_valid=n_valid, topk=topk)


def _dsa(qaT, qiT, wiT, ka, ki, vaT, *, nb, tq, variants, topk):
    nj = len(variants)
    s_max = ka.shape[1]
    qspec = lambda r: pl.BlockSpec((r, tq), lambda b, j: (0, b * nj + j))
    return pl.pallas_call(
        functools.partial(_dsa_kernel, tq=tq, variants=variants, topk=topk),
        grid=(nb, nj),
        in_specs=[qspec(512), qspec(512), qspec(8),
                  pl.BlockSpec((None, s_max, 128), lambda b, j: (b, 0, 0)),
                  pl.BlockSpec((None, s_max, 64), lambda b, j: (b, 0, 0)),
                  pl.BlockSpec((128, s_max), lambda b, j: (0, b))],
        out_specs=qspec(512),
        out_shape=jax.ShapeDtypeStruct((512, nb * nj * tq), F32),
        scratch_shapes=[pltpu.VMEM((s_max, tq), F32), pltpu.VMEM((s_max, tq), F32),
                        pltpu.VMEM((KVH_A * DH_A, H_A * tq), BF16),
                        pltpu.VMEM((_KEY_ROWS, H_A * tq), F32),
                        pltpu.VMEM((_KEY_ROWS, H_A * tq), BF16),
                        pltpu.VMEM((H_A, tq), F32), pltpu.VMEM((H_A, tq), F32)],
        compiler_params=_params("parallel", "arbitrary"),
        name="dsa",
    )(qaT, qiT, wiT, ka, ki, vaT)


def _dsa_few_kernel(qa_ref, qi_ref, wi_ref, ka_ref, ki_ref, va_ref, kan_ref, kin_ref, van_ref, o_ref,
                    *, nq, n_keys, qpos0, n_valid, topk):
    def against_keys(q_ref, cached_ref, new_ref):
        q = q_ref[...].astype(BF16)
        return jnp.concatenate(
            [lax.dot_general(q, cached_ref[...].astype(BF16), _NT, preferred_element_type=F32),
             lax.dot_general(q, new_ref[...].astype(BF16), _NT, preferred_element_type=F32)], axis=1)

    dots = against_keys(qi_ref, ki_ref, kin_ref)
    wi = wi_ref[...]
    score = jnp.zeros((nq, n_keys), F32)
    for h in range(H_IDX):
        score = score + wi[:, h:h + 1] * jnp.maximum(dots[nq * h:nq * (h + 1), :], 0.0)
    key_pos = lax.broadcasted_iota(I32, (nq, n_keys), 1)
    q_pos = qpos0 + lax.broadcasted_iota(I32, (nq, n_keys), 0)
    allowed = ((key_pos >> _CHUNK_SHIFT) <= (q_pos >> _CHUNK_SHIFT)) & (key_pos < n_valid)
    x = jnp.where(allowed, score, MASKED_SCORE)

    def count(pred):
        return jnp.sum(jnp.where(pred, 1.0, 0.0), axis=1, keepdims=True)

    t = _kth_largest_key(lambda piv: count(x >= piv), topk)
    lo = _key_float(t)
    hi = _key_float(t + 1)
    need = float(topk) - count(x >= hi)
    tie = (x >= lo) & (x < hi)
    tie_f = jnp.where(tie, 1.0, 0.0)
    tri = jnp.where(lax.broadcasted_iota(I32, (LANE, LANE), 0) < lax.broadcasted_iota(I32, (LANE, LANE), 1),
                    1.0, 0.0).astype(BF16)
    running = jnp.zeros((nq, 1), F32)
    before = []
    for c in range(n_keys // LANE):
        blk = tie_f[:, LANE * c:LANE * (c + 1)]
        before.append(running + jnp.dot(blk.astype(BF16), tri, preferred_element_type=F32))
        running = running + jnp.sum(blk, axis=1, keepdims=True)
    before = jnp.concatenate(before, axis=1)
    sel = ((x >= hi) | (tie & (before < need))) & allowed
    bias = jnp.where(sel, 0.0, NEG_BIG)

    s = against_keys(qa_ref, ka_ref, kan_ref) + jnp.concatenate([bias] * H_A, axis=0)
    m = jnp.max(s, axis=1, keepdims=True)
    p = jnp.exp(s - m)
    l = jnp.sum(p, axis=1, keepdims=True)
    pb = p.astype(BF16)
    n_cached = n_keys - LANE
    pv = (jnp.dot(pb[:, :n_cached], va_ref[...].astype(BF16), preferred_element_type=F32)
          + jnp.dot(pb[:, n_cached:], van_ref[...].astype(BF16), preferred_element_type=F32)) / l
    for h in range(H_A):
        g = h // GRP_A
        o_ref[:, DH_A * h:DH_A * (h + 1)] = pv[nq * h:nq * (h + 1), DH_A * g:DH_A * (g + 1)]


def _dsa_few(qa_rows, qi_rows, wi, ka, ki, va, ka_new, ki_new, va_new, *, nq, qpos0, n_valid, topk):
    nb, past = ka.shape[0], ka.shape[1]
    assert past % LANE == 0 and ka_new.shape[1] == LANE
    n_keys = past + LANE
    blk = lambda r, c: pl.BlockSpec((None, r, c), lambda b: (b, 0, 0))
    return pl.pallas_call(
        functools.partial(_dsa_few_kernel, nq=nq, n_keys=n_keys, qpos0=qpos0, n_valid=n_valid, topk=topk),
        grid=(nb,),
        in_specs=[blk(H_A * nq, 128), blk(H_IDX * nq, 64), blk(nq, H_IDX),
                  blk(past, 128), blk(past, 64), blk(past, 128),
                  blk(LANE, 128), blk(LANE, 64), blk(LANE, 128)],
        out_specs=pl.BlockSpec((nq, W_A), lambda b: (b, 0)),
        out_shape=jax.ShapeDtypeStruct((nb * nq, W_A), F32),
        compiler_params=_params("parallel"),
        name="dsa_few",
    )(qa_rows, qi_rows, wi, ka, ki, va, ka_new, ki_new, va_new)


def _ret_kernel(q_ref, k_ref, v_ref, g_ref, cos_ref, sin_ref, dmask_ref, rdec_ref, kdec_ref,
                sdec_ref, s0_ref, o_ref, sout_ref, state_ref, *, c_len, nsub):
    ci = pl.program_id(1)

    @pl.when(ci == 0)
    def _():
        state_ref[...] = s0_ref[...]

    lane_head = (lax.broadcasted_iota(I32, (1, 2 * LANE), 1) % LANE) // (DK_B // 2)
    row_head = (lax.broadcasted_iota(I32, (2 * LANE, DV_B), 0) % LANE) // (DK_B // 2)
    for sub in range(nsub):
        rows = slice(sub * c_len, (sub + 1) * c_len)
        q = q_ref[rows, :]
        k = k_ref[rows, :]
        v = v_ref[rows, :]
        g = g_ref[rows, :]
        cos = cos_ref[rows, :]
        sin = sin_ref[rows, :]
        q1, q2 = q[:, :LANE], q[:, LANE:]
        k1, k2 = k[:, :LANE], k[:, LANE:]
        qr = jnp.concatenate([q1 * cos - q2 * sin, q1 * sin + q2 * cos], axis=1)
        kr = jnp.concatenate([k1 * cos - k2 * sin, k1 * sin + k2 * cos], axis=1) * (DK_B ** -0.5)
        state = state_ref[...]
        upd = _bdot(kr * kdec_ref[...], v, _TN)
        new_state = sdec_ref[...] * state
        vb = v.astype(BF16)
        krb = kr.astype(BF16)
        sb = state.astype(BF16)
        for h in range(H_B):
            qm = jnp.where(lane_head == h, qr, 0.0).astype(BF16)
            qk = lax.dot_general(qm, krb, _NT, preferred_element_type=F32) * dmask_ref[h]
            intra = jnp.dot(qk.astype(BF16), vb[:, DV_B * h:DV_B * (h + 1)], preferred_element_type=F32)
            inter = jnp.dot(qm, sb, preferred_element_type=F32) * rdec_ref[h]
            o = intra + inter
            mu = jnp.mean(o, axis=-1, keepdims=True)
            d = o - mu
            var = jnp.mean(d * d, axis=-1, keepdims=True)
            gh = g[:, DV_B * h:DV_B * (h + 1)]
            silu = gh * (1.0 / (1.0 + jnp.exp(-gh)))
            o_ref[rows, DV_B * h:DV_B * (h + 1)] = silu * (d * lax.rsqrt(var + EPS))
            new_state = new_state + jnp.where(row_head == h, upd[:, DV_B * h:DV_B * (h + 1)], 0.0)
        state_ref[...] = new_state

    @pl.when(ci == pl.num_programs(1) - 1)
    def _():
        sout_ref[...] = state_ref[...]


def _retention_tables(pos, c_len):
    half = DK_B // 2
    lg = jnp.log(1.0 - 2.0 ** (-5.0 - jnp.arange(H_B, dtype=F32)))
    inv = 1.0 / (ROT_BASE ** jnp.linspace(0.0, 1.0, half, dtype=F32))
    ang = pos.astype(F32)[:, None] * inv[None, :]
    cos = jnp.tile(jnp.cos(ang), (1, H_B))
    sin = jnp.tile(jnp.sin(ang), (1, H_B))
    i = jnp.arange(c_len, dtype=F32)
    diff = i[:, None] - i[None, :]
    dmask = jnp.where(diff >= 0, jnp.exp(jnp.maximum(diff, 0.0)[None] * lg[:, None, None]), 0.0)
    rdec = jnp.broadcast_to(jnp.exp((i + 1.0)[None, :] * lg[:, None])[:, :, None], (H_B, c_len, DV_B))
    lane_head = (jnp.arange(2 * LANE) % LANE) // half
    kdec = jnp.exp((c_len - 1.0 - i)[:, None] * lg[lane_head][None, :])
    sdec = jnp.broadcast_to(jnp.exp(c_len * lg[lane_head])[:, None], (2 * LANE, DV_B))
    return cos, sin, dmask.astype(F32), rdec.astype(F32), kdec.astype(F32), sdec.astype(F32)


def _state_to_packed(s):
    b = s.shape[0]
    return s.reshape(b, H_B, 2, DK_B // 2, DV_B).transpose(0, 2, 1, 3, 4).reshape(b, 2 * LANE, DV_B)


def _state_from_packed(s):
    b = s.shape[0]
    return s.reshape(b, 2, H_B, DK_B // 2, DV_B).transpose(0, 2, 1, 3, 4).reshape(b, H_B, DK_B, DV_B)


def _retention(qb, kb, vb, gb, s0_packed, pos, *, nb, t_len, c_len, nsub):
    blk = c_len * nsub
    nc = t_len // blk
    cos, sin, dmask, rdec, kdec, sdec = _retention_tables(pos, c_len)
    tok = lambda c: pl.BlockSpec((blk, c), lambda b, i: (b * nc + i, 0))
    tab = pl.BlockSpec((blk, LANE), lambda b, i: (i, 0))
    full = lambda a: pl.BlockSpec(a.shape, lambda b, i: (0,) * a.ndim)
    st = pl.BlockSpec((None, 2 * LANE, DV_B), lambda b, i: (b, 0, 0))
    return pl.pallas_call(
        functools.partial(_ret_kernel, c_len=c_len, nsub=nsub),
        grid=(nb, nc),
        in_specs=[tok(256), tok(256), tok(512), tok(512), tab, tab,
                  full(dmask), full(rdec), full(kdec), full(sdec), st],
        out_specs=(tok(512), st),
        out_shape=(jax.ShapeDtypeStruct((nb * t_len, W_B), F32),
                   jax.ShapeDtypeStruct((nb, 2 * LANE, DV_B), F32)),
        scratch_shapes=[pltpu.VMEM((2 * LANE, DV_B), F32)],
        compiler_params=_params("parallel", "arbitrary"),
        name="retention",
    )(qb, kb, vb, gb, cos, sin, dmask, rdec, kdec, sdec, s0_packed)


def _outproj_kernel(x_ref, oaT_ref, ob_ref, woa_ref, wob_ref, g2_ref, wq_ref, k1_ref, k2_ref,
                    x1_ref, h2T_ref, s1T_ref, s2T_ref):
    mix = _bdot(oaT_ref[...], woa_ref[...], _TN) + _bdot(ob_ref[...], wob_ref[...])
    x1 = x_ref[...] + mix
    x1_ref[...] = x1
    h2 = x1 * lax.rsqrt(jnp.mean(x1 * x1, axis=-1, keepdims=True) + EPS) * g2_ref[...]
    h2T_ref[...] = h2.T.astype(BF16)
    q = _bdot(h2, wq_ref[...])
    k1 = k1_ref[...]
    k2 = k2_ref[...]
    for h in range(PEER_HEADS):
        base = 2 * D_PHALF * h
        s1T_ref[h] = _bdot(k1, q[:, base:base + D_PHALF], _NT) * LOG2E
        s2T_ref[h] = _bdot(k2, q[:, base + D_PHALF:base + 2 * D_PHALF], _NT) * LOG2E


def _outproj(x2d, oaT, ob, woa, wob, ln2_g, wq, k1, k2, tm):
    n = x2d.shape[0]
    assert n % tm == 0
    g2 = ln2_g.reshape(1, D_MODEL)
    row = lambda c: pl.BlockSpec((tm, c), lambda i: (i, 0))
    col = lambda r: pl.BlockSpec((r, tm), lambda i: (0, i))
    full = lambda a: pl.BlockSpec(a.shape, lambda i: (0,) * a.ndim)
    sc = pl.BlockSpec((PEER_HEADS, N_KEYS, tm), lambda i: (0, 0, i))
    return pl.pallas_call(
        _outproj_kernel,
        grid=(n // tm,),
        in_specs=[row(D_MODEL), col(W_A), row(W_B), full(woa), full(wob), full(g2), full(wq),
                  full(k1), full(k2)],
        out_specs=(row(D_MODEL), col(D_MODEL), sc, sc),
        out_shape=(jax.ShapeDtypeStruct((n, D_MODEL), F32),
                   jax.ShapeDtypeStruct((D_MODEL, n), BF16),
                   jax.ShapeDtypeStruct((PEER_HEADS, N_KEYS, n), F32),
                   jax.ShapeDtypeStruct((PEER_HEADS, N_KEYS, n), F32)),
        compiler_params=_params("parallel"),
        name="outproj",
    )(x2d, oaT, ob, woa, wob, g2, wq, k1, k2)


def _peer_sel_kernel(s1T_ref, s2T_ref, thr_ref, s1s_ref, a_ref, b_ref, cand_ref):
    def order(v, i, j):
        v[i], v[j] = jnp.maximum(v[i], v[j]), jnp.minimum(v[i], v[j])

    def bitonic_merge(v):
        for d in (8, 4, 2, 1):
            for i in range(PEER_TOPK):
                if i & d == 0:
                    order(v, i, i | d)

    def top16(s, dst_ref):
        v = [s[8 * i:8 * (i + 1), :] for i in range(PEER_TOPK)]
        for k in (2, 4, 8, 16):
            d = k // 2
            while d >= 1:
                for i in range(PEER_TOPK):
                    if i & d == 0:
                        if i & k == 0:
                            order(v, i, i | d)
                        else:
                            order(v, i | d, i)
                d //= 2
        for dist in (4, 2, 1):
            v = [jnp.maximum(v[i], pltpu.roll(v[PEER_TOPK - 1 - i], dist, axis=0)) for i in range(PEER_TOPK)]
            bitonic_merge(v)
        dst_ref[...] = jnp.concatenate([x[0:1, :] for x in v], axis=0)

    def pair_sums(a, b):
        sums = [a + b[0:1]] + [a[0:8] + b[j:j + 1] for j in range(1, 8)] + [a[0:1] + b[8:16]]
        return jnp.concatenate(sums, axis=0)

    def candidates(h, carry):
        top16(s1T_ref[h], a_ref.at[h])
        top16(s2T_ref[h], b_ref.at[h])
        cand_ref[h] = pair_sums(a_ref[h], b_ref[h])
        return carry

    lax.fori_loop(0, PEER_HEADS, candidates, 0)

    def count_ge(piv):
        rows = []
        for h in range(PEER_HEADS):
            p = piv if jnp.ndim(piv) == 0 else piv[h:h + 1, :]
            rows.append(jnp.sum(jnp.where(cand_ref[h] >= p, 1.0, 0.0), axis=0, keepdims=True))
        return jnp.concatenate(rows, axis=0)

    thr_all = _key_float(_kth_largest_key(count_ge, PEER_TOPK))
    for h in range(PEER_HEADS):
        cand = cand_ref[h]
        chosen = cand >= thr_all[h:h + 1, :]
        top = cand[0:1]
        z = jnp.sum(jnp.where(chosen, jnp.exp2(cand - top), 0.0), axis=0, keepdims=True)
        shift = top + jnp.log2(z)
        shifted = pair_sums(a_ref[h] - shift, b_ref[h])
        thr_ref[h] = jnp.min(jnp.where(chosen, shifted, jnp.inf), axis=0, keepdims=True)
        s1s_ref[h] = s1T_ref[h] - shift


def _peer_select(s1T, s2T, tl):
    n = s1T.shape[2]
    assert n % tl == 0
    sc = pl.BlockSpec((PEER_HEADS, N_KEYS, tl), lambda i: (0, 0, i))
    st = pl.BlockSpec((PEER_HEADS, 1, tl), lambda i: (0, 0, i))
    stat = jax.ShapeDtypeStruct((PEER_HEADS, 1, n), F32)
    return pl.pallas_call(
        _peer_sel_kernel,
        grid=(n // tl,),
        in_specs=[sc, sc],
        out_specs=(st, sc),
        out_shape=(stat, jax.ShapeDtypeStruct((PEER_HEADS, N_KEYS, n), F32)),
        scratch_shapes=[pltpu.VMEM((PEER_HEADS, PEER_TOPK, tl), F32),
                        pltpu.VMEM((PEER_HEADS, PEER_TOPK, tl), F32),
                        pltpu.VMEM((PEER_HEADS, 80, tl), F32)],
        compiler_params=_params("parallel"),
        name="peer_select",
    )(s1T, s2T)


_I1_PER_BLOCK = 16
_EXPERT_BLOCK = _I1_PER_BLOCK * N_KEYS
_MM_ROWS = 256
_SQRT_HALF = 0.7071067811865476


def _peer_dense_kernel(h2T_ref, u_ref, vT_ref, s1s_ref, s2T_ref, thr_ref,
                       x1_ref, gf_ref, y_ref, aT_ref, wT_ref, acc_ref, *, tm):
    j = pl.program_id(1)
    n_i1 = _I1_PER_BLOCK
    n_lt = tm // LANE
    n_half = 2 if n_lt % 2 == 0 else 1
    lt_per_half = n_lt // n_half
    half_lanes = [slice(hf * lt_per_half * LANE, (hf + 1) * lt_per_half * LANE) for hf in range(n_half)]

    @pl.when(j == 0)
    def _():
        acc_ref[...] = jnp.zeros_like(acc_ref)

    i1_base = pl.multiple_of(j * n_i1, n_i1)

    def act_piece(hf, r0, r1):
        aT_ref[r0:r1, half_lanes[hf]] = jnp.dot(u_ref[r0:r1, :], h2T_ref[:, half_lanes[hf]],
                                                preferred_element_type=F32)

    def out_piece(hf, r):
        rows = slice(_MM_ROWS * r, _MM_ROWS * (r + 1))
        acc_ref[rows, half_lanes[hf]] += jnp.dot(vT_ref[rows, :], wT_ref[:, half_lanes[hf]],
                                                 preferred_element_type=F32)

    def gate_group(hf, ii):
        rows = slice(N_KEYS * ii, N_KEYS * (ii + 1))
        for lt in range(lt_per_half):
            lanes = slice((hf * lt_per_half + lt) * LANE, (hf * lt_per_half + lt + 1) * LANE)
            gate = jnp.zeros((N_KEYS, LANE), F32)
            for h in range(PEER_HEADS):
                s1 = s1s_ref[h, pl.ds(i1_base, n_i1), lanes][ii:ii + 1]
                sm = s1 + s2T_ref[h, :, lanes]
                gate = gate + jnp.where(sm >= thr_ref[h, :, lanes], jnp.exp2(sm), 0.0)
            a = aT_ref[rows, lanes]
            act = 0.5 * a * (1.0 + lax.erf(a * _SQRT_HALF))
            wT_ref[rows, lanes] = (gate * act).astype(BF16)

    cuts = [0, N_KEYS, 2 * N_KEYS] + list(range(2 * N_KEYS + _MM_ROWS, _EXPERT_BLOCK + 1, _MM_ROWS))
    pending = [functools.partial(act_piece, 0, r0, r1) for r0, r1 in zip(cuts[:-1], cuts[1:])]
    pending += [functools.partial(act_piece, hf, r0, r0 + _MM_ROWS) for hf in range(1, n_half)
                for r0 in range(0, _EXPERT_BLOCK, _MM_ROWS)]
    pending.pop(0)()
    pending.pop(0)()
    for hf in range(n_half):
        for ii in range(n_i1):
            if pending and (hf, ii) != (0, 0):
                pending.pop(0)()
            gate_group(hf, ii)
        pending.extend(functools.partial(out_piece, hf, r) for r in range(D_MODEL // _MM_ROWS))
    for piece in pending:
        piece()

    @pl.when(j == pl.num_programs(1) - 1)
    def _():
        x2 = x1_ref[...] + acc_ref[...].T
        y_ref[...] = x2 * lax.rsqrt(jnp.mean(x2 * x2, axis=-1, keepdims=True) + EPS) * gf_ref[...]


def _peer_dense(h2T, u_bf, vT_bf, s1s, s2T, thr, x1, lnf_g, tm):
    n = x1.shape[0]
    assert n % tm == 0
    ne = u_bf.shape[0] // _EXPERT_BLOCK
    gf = lnf_g.reshape(1, D_MODEL)
    sc = pl.BlockSpec((PEER_HEADS, N_KEYS, tm), lambda i, j: (0, 0, i))
    st = pl.BlockSpec((PEER_HEADS, 1, tm), lambda i, j: (0, 0, i))
    return pl.pallas_call(
        functools.partial(_peer_dense_kernel, tm=tm),
        grid=(n // tm, ne),
        in_specs=[pl.BlockSpec((D_MODEL, tm), lambda i, j: (0, i)),
                  pl.BlockSpec((_EXPERT_BLOCK, D_MODEL), lambda i, j: (j, 0)),
                  pl.BlockSpec((D_MODEL, _EXPERT_BLOCK), lambda i, j: (0, j)),
                  sc, sc, st,
                  pl.BlockSpec((tm, D_MODEL), lambda i, j: (i, 0)),
                  pl.BlockSpec((1, D_MODEL), lambda i, j: (0, 0))],
        out_specs=pl.BlockSpec((tm, D_MODEL), lambda i, j: (i, 0)),
        out_shape=jax.ShapeDtypeStruct((n, D_MODEL), F32),
        scratch_shapes=[pltpu.VMEM((_EXPERT_BLOCK, tm), F32),
                        pltpu.VMEM((_EXPERT_BLOCK, tm), BF16),
                        pltpu.VMEM((D_MODEL, tm), F32)],
        compiler_params=_params("parallel", "arbitrary"),
        name="peer_dense",
    )(h2T, u_bf, vT_bf, s1s, s2T, thr, x1, gf)


def _mix_to_output(x2d, oaT, ob, lw, tm_out, tl_sel, tm_dense):
    x1, h2T, s1T, s2T = _outproj(x2d, oaT, ob, lw["woa"], lw["wob"], lw["ln2_g"], lw["wq"],
                                 lw["k1"], lw["k2"], tm_out)
    thr, s1s = _peer_select(s1T, s2T, tl_sel)
    return _peer_dense(h2T, lw["u"], lw["vT"], s1s, s2T, thr, x1, lw["lnf_g"], tm_dense)


def _layer_prompt(x, lw):
    nb, t_len, _ = x.shape
    n = nb * t_len
    x2d = x.reshape(n, D_MODEL)
    ka, va, ki, qb, kb, vb, gb, qaT, qiT, vaT, wiT = _inproj(x2d, lw["ln1_g"], lw["wstd"], lw["wt"], 512)
    tq = 256
    nj = t_len // tq
    variants = tuple((tq * (j + 1), tq * j, tq * (j + 1)) for j in range(nj))
    oaT = _dsa(qaT, qiT, wiT, ka.reshape(nb, t_len, 128), ki.reshape(nb, t_len, 64), vaT,
               nb=nb, tq=tq, variants=variants, topk=min(TOPK_MAX, t_len // 4))
    s0 = jnp.zeros((nb, 2 * LANE, DV_B), F32)
    ob, s_fin = _retention(qb, kb, vb, gb, s0, jnp.arange(t_len, dtype=I32),
                           nb=nb, t_len=t_len, c_len=CHUNK, nsub=4)
    y = _mix_to_output(x2d, oaT, ob, lw, 512, 256, 512)
    return (y.reshape(nb, t_len, D_MODEL), ka.reshape(nb, t_len, KVH_A, DH_A),
            va.reshape(nb, t_len, KVH_A, DH_A), ki.reshape(nb, t_len, D_IDX), _state_from_packed(s_fin))


def _layer_sample(x, ck, cv, cki, cs, lw):
    nb, t_len, _ = x.shape
    past = ck.shape[1]
    n = nb * t_len
    x2d = x.reshape(n, D_MODEL)
    ka, va, ki, qb, kb, vb, gb, qaT, qiT, vaT, wiT = _inproj(x2d, lw["ln1_g"], lw["wstd"], lw["wt"], n)
    total = past + t_len
    new_block = lambda a: jnp.pad(a.reshape(nb, t_len, -1), ((0, 0), (0, LANE - t_len), (0, 0)))
    per_head = lambda a, d: a.reshape(H_A, d, nb, t_len).transpose(2, 0, 3, 1)
    qa = per_head(qaT, DH_A) * (DH_A ** -0.5)
    group_cols = (jnp.arange(KVH_A * DH_A) // DH_A)[None, :] == (jnp.arange(H_A) // GRP_A)[:, None]
    qa_rows = jnp.where(group_cols[None, :, None, :], jnp.concatenate([qa] * KVH_A, axis=-1), 0.0)
    oa = _dsa_few(qa_rows.reshape(nb, H_A * t_len, KVH_A * DH_A),
                  per_head(qiT, D_IDX).reshape(nb, H_IDX * t_len, D_IDX),
                  wiT.reshape(H_IDX, nb, t_len).transpose(1, 2, 0),
                  ck.reshape(nb, past, KVH_A * DH_A), cki, cv.reshape(nb, past, KVH_A * DH_A),
                  new_block(ka), new_block(ki), new_block(va),
                  nq=t_len, qpos0=past, n_valid=total, topk=min(TOPK_MAX, total // 4))
    oaT = oa.T
    ob, s_new = _retention(qb, kb, vb, gb, _state_to_packed(cs), past + jnp.arange(t_len, dtype=I32),
                           nb=nb, t_len=t_len, c_len=t_len, nsub=1)
    y = _mix_to_output(x2d, oaT, ob, lw, n, n, n)
    return (y.reshape(nb, t_len, D_MODEL), ka.reshape(nb, t_len, KVH_A, DH_A),
            va.reshape(nb, t_len, KVH_A, DH_A), ki.reshape(nb, t_len, D_IDX), _state_from_packed(s_new))


def kernel(x_prompt, x_sample, cache_attn_k, cache_attn_v, cache_idx_k, state_retention,
           ln1_g, w_in, w_out, ln2_g, peer_w_query, peer_keys1, peer_keys2, peer_u, peer_v, ln_final_g):
    depth = w_in.shape[0]
    assert depth == 1, "the final norm is fused into the (single) layer's last kernel"
    wstd, wt = _prep_inproj_weights(w_in[0])
    lw = dict(ln1_g=ln1_g[0], wstd=wstd, wt=wt,
              woa=w_out[0, :W_A].astype(BF16), wob=w_out[0, W_A:].astype(BF16),
              ln2_g=ln2_g[0], wq=peer_w_query[0].astype(BF16),
              k1=peer_keys1[0], k2=peer_keys2[0],
              u=peer_u[0].astype(BF16), vT=peer_v[0].T.astype(BF16), lnf_g=ln_final_g)
    yp, kp, vp, kip, sp = _layer_prompt(x_prompt, lw)
    ys, ks, vs, kis, ss = _layer_sample(x_sample, cache_attn_k[0], cache_attn_v[0], cache_idx_k[0],
                                        state_retention[0], lw)
    st = lambda a: a[None]
    return (yp, ys, st(kp), st(vp), st(kip), st(sp), st(ks), st(vs), st(kis), st(ss))
```

```python
import functools

import jax
import jax.numpy as jnp
from jax import lax
from jax.experimental import pallas as pl
from jax.experimental.pallas import tpu as pltpu

F32 = jnp.float32
BF16 = jnp.bfloat16
I32 = jnp.int32

D_MODEL = 1024
CHUNK = 64
_CHUNK_SHIFT = 6
assert 1 << _CHUNK_SHIFT == CHUNK
EPS = 1e-6
H_A, KVH_A, DH_A = 8, 2, 64
GRP_A = H_A // KVH_A
H_IDX, D_IDX = 8, 64
TOPK_MAX = 256
H_B, DK_B, DV_B = 4, 64, 128
ROT_BASE = 10000.0
W_A = H_A * DH_A
W_B = H_B * DV_B
N_KEYS = 128
PEER_HEADS = 8
PEER_TOPK = 16
D_PHALF = 128

LOG2E = 1.4426950408889634
INT_MIN = -(2 ** 31)
NEG_BIG = -1e30
MASKED_SCORE = -3.0e38
LANE = 128
VMEM_LIMIT_BYTES = 56 * 1024 * 1024

_NT = (((1,), (1,)), ((), ()))
_TN = (((0,), (0,)), ((), ()))


def _bdot(a, b, dims=None):
    a = a.astype(BF16)
    b = b.astype(BF16)
    if dims is None:
        return jnp.dot(a, b, preferred_element_type=F32)
    return lax.dot_general(a, b, dims, preferred_element_type=F32)


def _params(*sem):
    return pltpu.CompilerParams(dimension_semantics=sem, vmem_limit_bytes=VMEM_LIMIT_BYTES)


def _key_float(k):
    return pltpu.bitcast(k ^ ((k >> 31) & 0x7FFFFFFF), F32)


def _kth_largest_key(count_ge, k):
    kf = float(k)
    t0 = jnp.where(count_ge(0.0) >= kf, 0, INT_MIN).astype(I32)

    def bit_body(i, t):
        cand = t | lax.shift_left(jnp.int32(1), 30 - i)
        return jnp.where(count_ge(_key_float(cand)) >= kf, cand, t)

    return lax.fori_loop(0, 31, bit_body, t0)


def _inproj_kernel(x_ref, g_ref, wstd_ref, wt_ref,
                   ka_ref, va_ref, ki_ref, qb_ref, kb_ref, vb_ref, gb_ref,
                   qaT_ref, qiT_ref, vaT_ref, wiT_ref):
    x = x_ref[...]
    h = x * lax.rsqrt(jnp.mean(x * x, axis=-1, keepdims=True) + EPS) * g_ref[...]
    hb = h.astype(BF16)
    std = jnp.dot(hb, wstd_ref[...], preferred_element_type=F32)
    ka_ref[...] = std[:, 0:128]
    va_ref[...] = std[:, 128:256]
    ki_ref[...] = std[:, 256:320]
    qb_ref[...] = std[:, 384:640]
    kb_ref[...] = std[:, 640:896]
    vb_ref[...] = std[:, 896:1408]
    gb_ref[...] = std[:, 1408:1920]
    t = lax.dot_general(wt_ref[...], hb, _NT, preferred_element_type=F32)
    qaT_ref[...] = t[0:512]
    qiT_ref[...] = t[512:1024]
    vaT_ref[...] = t[1024:1152]
    wiT_ref[...] = t[1152:1160]


def _retention_perm():
    idx = []
    for half in range(2):
        for h in range(H_B):
            for d in range(DK_B // 2):
                idx.append(h * DK_B + half * (DK_B // 2) + d)
    return jnp.asarray(idx, dtype=I32)


def _prep_inproj_weights(w_in):
    o = [0]
    for s in (W_A, KVH_A * DH_A, KVH_A * DH_A, H_IDX * D_IDX, D_IDX, H_IDX,
              H_B * DK_B, H_B * DK_B, W_B, W_B):
        o.append(o[-1] + s)
    q_a, k_a, v_a, q_i, k_i, w_i, q_b, k_b, v_b, g_b = [w_in[:, o[i]:o[i + 1]] for i in range(10)]
    perm = _retention_perm()
    pad = jnp.zeros((D_MODEL, 64), w_in.dtype)
    wstd = jnp.concatenate([k_a, v_a, k_i, pad, q_b[:, perm], k_b[:, perm], v_b, g_b], axis=1)
    wt = jnp.concatenate([q_a, q_i, v_a, w_i], axis=1).T
    return wstd.astype(BF16), wt.astype(BF16)


def _inproj(x2d, ln_g, wstd, wt, tm):
    n = x2d.shape[0]
    assert n % tm == 0
    row = lambda c: pl.BlockSpec((tm, c), lambda i: (i, 0))
    col = lambda r: pl.BlockSpec((r, tm), lambda i: (0, i))
    full = lambda a: pl.BlockSpec(a.shape, lambda i: (0,) * a.ndim)
    g2 = ln_g.reshape(1, D_MODEL)
    out_shape = (
        jax.ShapeDtypeStruct((n, 128), F32), jax.ShapeDtypeStruct((n, 128), F32),
        jax.ShapeDtypeStruct((n, 64), F32),
        jax.ShapeDtypeStruct((n, 256), F32), jax.ShapeDtypeStruct((n, 256), F32),
        jax.ShapeDtypeStruct((n, 512), F32), jax.ShapeDtypeStruct((n, 512), F32),
        jax.ShapeDtypeStruct((512, n), F32), jax.ShapeDtypeStruct((512, n), F32),
        jax.ShapeDtypeStruct((128, n), F32), jax.ShapeDtypeStruct((8, n), F32),
    )
    out_specs = (row(128), row(128), row(64), row(256), row(256), row(512), row(512),
                 col(512), col(512), col(128), col(8))
    return pl.pallas_call(
        _inproj_kernel,
        grid=(n // tm,),
        in_specs=[row(D_MODEL), full(g2), full(wstd), full(wt)],
        out_specs=out_specs,
        out_shape=out_shape,
        compiler_params=_params("parallel"),
        name="inproj",
    )(x2d, g2, wstd, wt)


_KEY_ROWS = 256


def _dsa_body(qaT_ref, qiT_ref, wiT_ref, ka_ref, ki_ref, vaT_ref, oT_ref, score_ref, bias_ref,
              qpad_ref, s_ref, p_ref, m_ref, l_ref,
              *, n_keys, tq, qpos0, n_valid, topk):
    ch = _KEY_ROWS
    nch = n_keys // ch
    row_iota = lax.broadcasted_iota(I32, (ch, tq), 0)
    lane_iota = lax.broadcasted_iota(I32, (ch, tq), 1)
    q_chunk = (qpos0 + lane_iota) >> _CHUNK_SHIFT
    wi = wiT_ref[...]

    def allowed(srow):
        return ((srow >> _CHUNK_SHIFT) <= q_chunk) & (srow < n_valid)

    qi = [qiT_ref[D_IDX * h:D_IDX * (h + 1), :].astype(BF16) for h in range(H_IDX)]

    def score_chunk(c, carry):
        r0 = pl.multiple_of(c * ch, ch)
        kic = ki_ref[pl.ds(r0, ch), :].astype(BF16)
        acc = jnp.zeros((ch, tq), F32)
        for h in range(H_IDX):
            d = jnp.dot(kic, qi[h], preferred_element_type=F32)
            acc = acc + wi[h:h + 1, :] * jnp.maximum(d, 0.0)
        score_ref[pl.ds(r0, ch), :] = jnp.where(allowed(r0 + row_iota), acc, MASKED_SCORE)
        return carry

    lax.fori_loop(0, nch, score_chunk, 0)

    def count(pred):
        def body(c, acc):
            r0 = pl.multiple_of(c * ch, ch)
            x = score_ref[pl.ds(r0, ch), :]
            hit = jnp.where(pred(x, r0 + row_iota), 1.0, 0.0)
            return acc + jnp.sum(hit.reshape(ch // 8, 8, tq), axis=0)
        acc = lax.fori_loop(0, nch, body, jnp.zeros((8, tq), F32), unroll=min(nch, 4))
        return jnp.sum(acc, axis=0, keepdims=True)

    t = _kth_largest_key(lambda piv: count(lambda x, r: x >= piv), topk)
    lo = _key_float(t)
    hi = _key_float(t + 1)
    need = float(topk) - count(lambda x, r: x >= hi)

    tri = jnp.where(lax.broadcasted_iota(I32, (ch, ch), 0) > lax.broadcasted_iota(I32, (ch, ch), 1),
                    1.0, 0.0).astype(BF16)

    def bias_chunk(c, ties_before):
        r0 = pl.multiple_of(c * ch, ch)
        x = score_ref[pl.ds(r0, ch), :]
        tie = (x >= lo) & (x < hi)
        tie_f = jnp.where(tie, 1.0, 0.0)
        before = ties_before + jnp.dot(tri, tie_f.astype(BF16), preferred_element_type=F32)
        sel = ((x >= hi) | (tie & (before < need))) & allowed(r0 + row_iota)
        bias_ref[pl.ds(r0, ch), :] = jnp.where(sel, 0.0, NEG_BIG)
        return ties_before + jnp.sum(tie_f, axis=0, keepdims=True)

    lax.fori_loop(0, nch, bias_chunk, jnp.zeros((1, tq), F32))

    zeros_half = jnp.zeros((DH_A, tq), F32)
    scale = DH_A ** -0.5
    for h in range(H_A):
        qh = qaT_ref[DH_A * h:DH_A * (h + 1), :] * scale
        qpad = (jnp.concatenate([qh, zeros_half], axis=0) if h // GRP_A == 0
                else jnp.concatenate([zeros_half, qh], axis=0))
        qpad_ref[:, tq * h:tq * (h + 1)] = qpad.astype(BF16)
    m_ref[...] = jnp.full(m_ref.shape, NEG_BIG, F32)
    l_ref[...] = jnp.zeros(l_ref.shape, F32)
    oT_ref[...] = jnp.zeros(oT_ref.shape, F32)

    def att_chunk(c, carry):
        r0 = pl.multiple_of(c * ch, ch)
        kc = ka_ref[pl.ds(r0, ch), :].astype(BF16)
        bias = bias_ref[pl.ds(r0, ch), :]
        s_ref[...] = jnp.dot(kc, qpad_ref[...], preferred_element_type=F32)
        alphas = []
        for h in range(H_A):
            lanes = slice(tq * h, tq * (h + 1))
            s = s_ref[:, lanes] + bias
            m_old = m_ref[h:h + 1, :]
            m_new = jnp.maximum(m_old, jnp.max(s, axis=0, keepdims=True))
            alpha = jnp.exp(m_old - m_new)
            p = jnp.exp(s - m_new)
            l_ref[h:h + 1, :] = l_ref[h:h + 1, :] * alpha + jnp.sum(p, axis=0, keepdims=True)
            p_ref[:, lanes] = p.astype(BF16)
            m_ref[h:h + 1, :] = m_new
            alphas.append(alpha)
        for g in range(KVH_A):
            vc = vaT_ref[DH_A * g:DH_A * (g + 1), pl.ds(r0, ch)].astype(BF16)
            pv = jnp.dot(vc, p_ref[:, GRP_A * tq * g:GRP_A * tq * (g + 1)],
                         preferred_element_type=F32)
            for hh in range(GRP_A):
                h = GRP_A * g + hh
                rows = slice(DH_A * h, DH_A * (h + 1))
                oT_ref[rows, :] = oT_ref[rows, :] * alphas[h] + pv[:, tq * hh:tq * (hh + 1)]
        return carry

    lax.fori_loop(0, nch, att_chunk, 0)
    for h in range(H_A):
        rows = slice(DH_A * h, DH_A * (h + 1))
        oT_ref[rows, :] = oT_ref[rows, :] / l_ref[h:h + 1, :]


def _dsa_kernel(qaT_ref, qiT_ref, wiT_ref, ka_ref, ki_ref, vaT_ref, oT_ref, score_ref, bias_ref,
              qpad_ref, s_ref, p_ref, m_ref, l_ref,
                *, tq, variants, topk):
    j = pl.program_id(1)
    for jj, (n_keys, qpos0, n_valid) in enumerate(variants):
        @pl.when(j == jj)
        def _(n_keys=n_keys, qpos0=qpos0, n_valid=n_valid):
            _dsa_body(qaT_ref, qiT_ref, wiT_ref, ka_ref, ki_ref, vaT_ref, oT_ref, score_ref, bias_ref,
              qpad_ref, s_ref, p_ref, m_ref, l_ref,
                      n_keys=n_keys, tq=tq, qpos0=qpos0, n_valid=n_valid, topk=topk)


def _dsa(qaT, qiT, wiT, ka, ki, vaT, *, nb, tq, variants, topk):
    nj = len(variants)
    s_max = ka.shape[1]
    qspec = lambda r: pl.BlockSpec((r, tq), lambda b, j: (0, b * nj + j))
    return pl.pallas_call(
        functools.partial(_dsa_kernel, tq=tq, variants=variants, topk=topk),
        grid=(nb, nj),
        in_specs=[qspec(512), qspec(512), qspec(8),
                  pl.BlockSpec((None, s_max, 128), lambda b, j: (b, 0, 0)),
                  pl.BlockSpec((None, s_max, 64), lambda b, j: (b, 0, 0)),
                  pl.BlockSpec((128, s_max), lambda b, j: (0, b))],
        out_specs=qspec(512),
        out_shape=jax.ShapeDtypeStruct((512, nb * nj * tq), F32),
        scratch_shapes=[pltpu.VMEM((s_max, tq), F32), pltpu.VMEM((s_max, tq), F32),
                        pltpu.VMEM((KVH_A * DH_A, H_A * tq), BF16),
                        pltpu.VMEM((_KEY_ROWS, H_A * tq), F32),
                        pltpu.VMEM((_KEY_ROWS, H_A * tq), BF16),
                        pltpu.VMEM((H_A, tq), F32), pltpu.VMEM((H_A, tq), F32)],
        compiler_params=_params("parallel", "arbitrary"),
        name="dsa",
    )(qaT, qiT, wiT, ka, ki, vaT)


def _dsa_few_kernel(qa_ref, qi_ref, wi_ref, ka_ref, ki_ref, va_ref, kan_ref, kin_ref, van_ref, o_ref,
                    *, nq, n_keys, qpos0, n_valid, topk):
    def against_keys(q_ref, cached_ref, new_ref):
        q = q_ref[...].astype(BF16)
        return jnp.concatenate(
            [lax.dot_general(q, cached_ref[...].astype(BF16), _NT, preferred_element_type=F32),
             lax.dot_general(q, new_ref[...].astype(BF16), _NT, preferred_element_type=F32)], axis=1)

    dots = against_keys(qi_ref, ki_ref, kin_ref)
    wi = wi_ref[...]
    score = jnp.zeros((nq, n_keys), F32)
    for h in range(H_IDX):
        score = score + wi[:, h:h + 1] * jnp.maximum(dots[nq * h:nq * (h + 1), :], 0.0)
    key_pos = lax.broadcasted_iota(I32, (nq, n_keys), 1)
    q_pos = qpos0 + lax.broadcasted_iota(I32, (nq, n_keys), 0)
    allowed = ((key_pos >> _CHUNK_SHIFT) <= (q_pos >> _CHUNK_SHIFT)) & (key_pos < n_valid)
    x = jnp.where(allowed, score, MASKED_SCORE)

    def count(pred):
        return jnp.sum(jnp.where(pred, 1.0, 0.0), axis=1, keepdims=True)

    t = _kth_largest_key(lambda piv: count(x >= piv), topk)
    lo = _key_float(t)
    hi = _key_float(t + 1)
    need = float(topk) - count(x >= hi)
    tie = (x >= lo) & (x < hi)
    tie_f = jnp.where(tie, 1.0, 0.0)
    tri = jnp.where(lax.broadcasted_iota(I32, (LANE, LANE), 0) < lax.broadcasted_iota(I32, (LANE, LANE), 1),
                    1.0, 0.0).astype(BF16)
    running = jnp.zeros((nq, 1), F32)
    before = []
    for c in range(n_keys // LANE):
        blk = tie_f[:, LANE * c:LANE * (c + 1)]
        before.append(running + jnp.dot(blk.astype(BF16), tri, preferred_element_type=F32))
        running = running + jnp.sum(blk, axis=1, keepdims=True)
    before = jnp.concatenate(before, axis=1)
    sel = ((x >= hi) | (tie & (before < need))) & allowed
    bias = jnp.where(sel, 0.0, NEG_BIG)

    s = against_keys(qa_ref, ka_ref, kan_ref) + jnp.concatenate([bias] * H_A, axis=0)
    m = jnp.max(s, axis=1, keepdims=True)
    p = jnp.exp(s - m)
    l = jnp.sum(p, axis=1, keepdims=True)
    pb = p.astype(BF16)
    n_cached = n_keys - LANE
    pv = (jnp.dot(pb[:, :n_cached], va_ref[...].astype(BF16), preferred_element_type=F32)
          + jnp.dot(pb[:, n_cached:], van_ref[...].astype(BF16), preferred_element_type=F32)) / l
    for h in range(H_A):
        g = h // GRP_A
        o_ref[:, DH_A * h:DH_A * (h + 1)] = pv[nq * h:nq * (h + 1), DH_A * g:DH_A * (g + 1)]


def _dsa_few(qa_rows, qi_rows, wi, ka, ki, va, ka_new, ki_new, va_new, *, nq, qpos0, n_valid, topk):
    nb, past = ka.shape[0], ka.shape[1]
    assert past % LANE == 0 and ka_new.shape[1] == LANE
    n_keys = past + LANE
    blk = lambda r, c: pl.BlockSpec((None, r, c), lambda b: (b, 0, 0))
    return pl.pallas_call(
        functools.partial(_dsa_few_kernel, nq=nq, n_keys=n_keys, qpos0=qpos0, n_valid=n_valid, topk=topk),
        grid=(nb,),
        in_specs=[blk(H_A * nq, 128), blk(H_IDX * nq, 64), blk(nq, H_IDX),
                  blk(past, 128), blk(past, 64), blk(past, 128),
                  blk(LANE, 128), blk(LANE, 64), blk(LANE, 128)],
        out_specs=pl.BlockSpec((nq, W_A), lambda b: (b, 0)),
        out_shape=jax.ShapeDtypeStruct((nb * nq, W_A), F32),
        compiler_params=_params("parallel"),
        name="dsa_few",
    )(qa_rows, qi_rows, wi, ka, ki, va, ka_new, ki_new, va_new)


def _ret_kernel(q_ref, k_ref, v_ref, g_ref, cos_ref, sin_ref, dmask_ref, rdec_ref, kdec_ref,
                sdec_ref, s0_ref, o_ref, sout_ref, state_ref, *, c_len, nsub):
    ci = pl.program_id(1)

    @pl.when(ci == 0)
    def _():
        state_ref[...] = s0_ref[...]

    lane_head = (lax.broadcasted_iota(I32, (1, 2 * LANE), 1) % LANE) // (DK_B // 2)
    row_head = (lax.broadcasted_iota(I32, (2 * LANE, DV_B), 0) % LANE) // (DK_B // 2)
    for sub in range(nsub):
        rows = slice(sub * c_len, (sub + 1) * c_len)
        q = q_ref[rows, :]
        k = k_ref[rows, :]
        v = v_ref[rows, :]
        g = g_ref[rows, :]
        cos = cos_ref[rows, :]
        sin = sin_ref[rows, :]
        q1, q2 = q[:, :LANE], q[:, LANE:]
        k1, k2 = k[:, :LANE], k[:, LANE:]
        qr = jnp.concatenate([q1 * cos - q2 * sin, q1 * sin + q2 * cos], axis=1)
        kr = jnp.concatenate([k1 * cos - k2 * sin, k1 * sin + k2 * cos], axis=1) * (DK_B ** -0.5)
        state = state_ref[...]
        upd = _bdot(kr * kdec_ref[...], v, _TN)
        new_state = sdec_ref[...] * state
        vb = v.astype(BF16)
        krb = kr.astype(BF16)
        sb = state.astype(BF16)
        for h in range(H_B):
            qm = jnp.where(lane_head == h, qr, 0.0).astype(BF16)
            qk = lax.dot_general(qm, krb, _NT, preferred_element_type=F32) * dmask_ref[h]
            intra = jnp.dot(qk.astype(BF16), vb[:, DV_B * h:DV_B * (h + 1)], preferred_element_type=F32)
            inter = jnp.dot(qm, sb, preferred_element_type=F32) * rdec_ref[h]
            o = intra + inter
            mu = jnp.mean(o, axis=-1, keepdims=True)
            d = o - mu
            var = jnp.mean(d * d, axis=-1, keepdims=True)
            gh = g[:, DV_B * h:DV_B * (h + 1)]
            silu = gh * (1.0 / (1.0 + jnp.exp(-gh)))
            o_ref[rows, DV_B * h:DV_B * (h + 1)] = silu * (d * lax.rsqrt(var + EPS))
            new_state = new_state + jnp.where(row_head == h, upd[:, DV_B * h:DV_B * (h + 1)], 0.0)
        state_ref[...] = new_state

    @pl.when(ci == pl.num_programs(1) - 1)
    def _():
        sout_ref[...] = state_ref[...]


def _retention_tables(pos, c_len):
    half = DK_B // 2
    lg = jnp.log(1.0 - 2.0 ** (-5.0 - jnp.arange(H_B, dtype=F32)))
    inv = 1.0 / (ROT_BASE ** jnp.linspace(0.0, 1.0, half, dtype=F32))
    ang = pos.astype(F32)[:, None] * inv[None, :]
    cos = jnp.tile(jnp.cos(ang), (1, H_B))
    sin = jnp.tile(jnp.sin(ang), (1, H_B))
    i = jnp.arange(c_len, dtype=F32)
    diff = i[:, None] - i[None, :]
    dmask = jnp.where(diff >= 0, jnp.exp(jnp.maximum(diff, 0.0)[None] * lg[:, None, None]), 0.0)
    rdec = jnp.broadcast_to(jnp.exp((i + 1.0)[None, :] * lg[:, None])[:, :, None], (H_B, c_len, DV_B))
    lane_head = (jnp.arange(2 * LANE) % LANE) // half
    kdec = jnp.exp((c_len - 1.0 - i)[:, None] * lg[lane_head][None, :])
    sdec = jnp.broadcast_to(jnp.exp(c_len * lg[lane_head])[:, None], (2 * LANE, DV_B))
    return cos, sin, dmask.astype(F32), rdec.astype(F32), kdec.astype(F32), sdec.astype(F32)


def _state_to_packed(s):
    b = s.shape[0]
    return s.reshape(b, H_B, 2, DK_B // 2, DV_B).transpose(0, 2, 1, 3, 4).reshape(b, 2 * LANE, DV_B)


def _state_from_packed(s):
    b = s.shape[0]
    return s.reshape(b, 2, H_B, DK_B // 2, DV_B).transpose(0, 2, 1, 3, 4).reshape(b, H_B, DK_B, DV_B)


def _retention(qb, kb, vb, gb, s0_packed, pos, *, nb, t_len, c_len, nsub):
    blk = c_len * nsub
    nc = t_len // blk
    cos, sin, dmask, rdec, kdec, sdec = _retention_tables(pos, c_len)
    tok = lambda c: pl.BlockSpec((blk, c), lambda b, i: (b * nc + i, 0))
    tab = pl.BlockSpec((blk, LANE), lambda b, i: (i, 0))
    full = lambda a: pl.BlockSpec(a.shape, lambda b, i: (0,) * a.ndim)
    st = pl.BlockSpec((None, 2 * LANE, DV_B), lambda b, i: (b, 0, 0))
    return pl.pallas_call(
        functools.partial(_ret_kernel, c_len=c_len, nsub=nsub),
        grid=(nb, nc),
        in_specs=[tok(256), tok(256), tok(512), tok(512), tab, tab,
                  full(dmask), full(rdec), full(kdec), full(sdec), st],
        out_specs=(tok(512), st),
        out_shape=(jax.ShapeDtypeStruct((nb * t_len, W_B), F32),
                   jax.ShapeDtypeStruct((nb, 2 * LANE, DV_B), F32)),
        scratch_shapes=[pltpu.VMEM((2 * LANE, DV_B), F32)],
        compiler_params=_params("parallel", "arbitrary"),
        name="retention",
    )(qb, kb, vb, gb, cos, sin, dmask, rdec, kdec, sdec, s0_packed)


def _outproj_kernel(x_ref, oaT_ref, ob_ref, woa_ref, wob_ref, g2_ref, wq_ref, k1_ref, k2_ref,
                    x1_ref, h2T_ref, s1T_ref, s2T_ref):
    mix = _bdot(oaT_ref[...], woa_ref[...], _TN) + _bdot(ob_ref[...], wob_ref[...])
    x1 = x_ref[...] + mix
    x1_ref[...] = x1
    h2 = x1 * lax.rsqrt(jnp.mean(x1 * x1, axis=-1, keepdims=True) + EPS) * g2_ref[...]
    h2T_ref[...] = h2.T.astype(BF16)
    q = _bdot(h2, wq_ref[...])
    k1 = k1_ref[...]
    k2 = k2_ref[...]
    for h in range(PEER_HEADS):
        base = 2 * D_PHALF * h
        s1T_ref[h] = _bdot(k1, q[:, base:base + D_PHALF], _NT) * LOG2E
        s2T_ref[h] = _bdot(k2, q[:, base + D_PHALF:base + 2 * D_PHALF], _NT) * LOG2E


def _outproj(x2d, oaT, ob, woa, wob, ln2_g, wq, k1, k2, tm):
    n = x2d.shape[0]
    assert n % tm == 0
    g2 = ln2_g.reshape(1, D_MODEL)
    row = lambda c: pl.BlockSpec((tm, c), lambda i: (i, 0))
    col = lambda r: pl.BlockSpec((r, tm), lambda i: (0, i))
    full = lambda a: pl.BlockSpec(a.shape, lambda i: (0,) * a.ndim)
    sc = pl.BlockSpec((PEER_HEADS, N_KEYS, tm), lambda i: (0, 0, i))
    return pl.pallas_call(
        _outproj_kernel,
        grid=(n // tm,),
        in_specs=[row(D_MODEL), col(W_A), row(W_B), full(woa), full(wob), full(g2), full(wq),
                  full(k1), full(k2)],
        out_specs=(row(D_MODEL), col(D_MODEL), sc, sc),
        out_shape=(jax.ShapeDtypeStruct((n, D_MODEL), F32),
                   jax.ShapeDtypeStruct((D_MODEL, n), BF16),
                   jax.ShapeDtypeStruct((PEER_HEADS, N_KEYS, n), F32),
                   jax.ShapeDtypeStruct((PEER_HEADS, N_KEYS, n), F32)),
        compiler_params=_params("parallel"),
        name="outproj",
    )(x2d, oaT, ob, woa, wob, g2, wq, k1, k2)


def _peer_sel_kernel(s1T_ref, s2T_ref, thr_ref, s1s_ref, a_ref, b_ref, cand_ref):
    def order(v, i, j):
        v[i], v[j] = jnp.maximum(v[i], v[j]), jnp.minimum(v[i], v[j])

    def bitonic_merge(v):
        for d in (8, 4, 2, 1):
            for i in range(PEER_TOPK):
                if i & d == 0:
                    order(v, i, i | d)

    def top16(s, dst_ref):
        v = [s[8 * i:8 * (i + 1), :] for i in range(PEER_TOPK)]
        for k in (2, 4, 8, 16):
            d = k // 2
            while d >= 1:
                for i in range(PEER_TOPK):
                    if i & d == 0:
                        if i & k == 0:
                            order(v, i, i | d)
                        else:
                            order(v, i | d, i)
                d //= 2
        for dist in (4, 2, 1):
            v = [jnp.maximum(v[i], pltpu.roll(v[PEER_TOPK - 1 - i], dist, axis=0)) for i in range(PEER_TOPK)]
            bitonic_merge(v)
        dst_ref[...] = jnp.concatenate([x[0:1, :] for x in v], axis=0)

    def pair_sums(a, b):
        sums = [a + b[0:1]] + [a[0:8] + b[j:j + 1] for j in range(1, 8)] + [a[0:1] + b[8:16]]
        return jnp.concatenate(sums, axis=0)

    def candidates(h, carry):
        top16(s1T_ref[h], a_ref.at[h])
        top16(s2T_ref[h], b_ref.at[h])
        cand_ref[h] = pair_sums(a_ref[h], b_ref[h])
        return carry

    lax.fori_loop(0, PEER_HEADS, candidates, 0)

    def count_ge(piv):
        rows = []
        for h in range(PEER_HEADS):
            p = piv if jnp.ndim(piv) == 0 else piv[h:h + 1, :]
            rows.append(jnp.sum(jnp.where(cand_ref[h] >= p, 1.0, 0.0), axis=0, keepdims=True))
        return jnp.concatenate(rows, axis=0)

    thr_all = _key_float(_kth_largest_key(count_ge, PEER_TOPK))
    for h in range(PEER_HEADS):
        cand = cand_ref[h]
        chosen = cand >= thr_all[h:h + 1, :]
        top = cand[0:1]
        z = jnp.sum(jnp.where(chosen, jnp.exp2(cand - top), 0.0), axis=0, keepdims=True)
        shift = top + jnp.log2(z)
        shifted = pair_sums(a_ref[h] - shift, b_ref[h])
        thr_ref[h] = jnp.min(jnp.where(chosen, shifted, jnp.inf), axis=0, keepdims=True)
        s1s_ref[h] = s1T_ref[h] - shift


def _peer_select(s1T, s2T, tl):
    n = s1T.shape[2]
    assert n % tl == 0
    sc = pl.BlockSpec((PEER_HEADS, N_KEYS, tl), lambda i: (0, 0, i))
    st = pl.BlockSpec((PEER_HEADS, 1, tl), lambda i: (0, 0, i))
    stat = jax.ShapeDtypeStruct((PEER_HEADS, 1, n), F32)
    return pl.pallas_call(
        _peer_sel_kernel,
        grid=(n // tl,),
        in_specs=[sc, sc],
        out_specs=(st, sc),
        out_shape=(stat, jax.ShapeDtypeStruct((PEER_HEADS, N_KEYS, n), F32)),
        scratch_shapes=[pltpu.VMEM((PEER_HEADS, PEER_TOPK, tl), F32),
                        pltpu.VMEM((PEER_HEADS, PEER_TOPK, tl), F32),
                        pltpu.VMEM((PEER_HEADS, 80, tl), F32)],
        compiler_params=_params("parallel"),
        name="peer_select",
    )(s1T, s2T)


_I1_PER_BLOCK = 16
_EXPERT_BLOCK = _I1_PER_BLOCK * N_KEYS
_MM_ROWS = 512
_SQRT_HALF = 0.7071067811865476


def _peer_dense_kernel(h2T_ref, u_ref, vT_ref, s1s_ref, s2T_ref, thr_ref,
                       x1_ref, gf_ref, y_ref, aT_ref, wT_ref, acc_ref, *, tm):
    j = pl.program_id(1)
    n_i1 = _I1_PER_BLOCK
    n_lt = tm // LANE
    n_half = 2 if n_lt % 2 == 0 else 1
    lt_per_half = n_lt // n_half
    half_lanes = [slice(hf * lt_per_half * LANE, (hf + 1) * lt_per_half * LANE) for hf in range(n_half)]

    @pl.when(j == 0)
    def _():
        acc_ref[...] = jnp.zeros_like(acc_ref)

    i1_base = pl.multiple_of(j * n_i1, n_i1)

    def act_piece(hf, r0, r1):
        aT_ref[r0:r1, half_lanes[hf]] = jnp.dot(u_ref[r0:r1, :], h2T_ref[:, half_lanes[hf]],
                                                preferred_element_type=F32)

    def out_piece(hf, r):
        rows = slice(_MM_ROWS * r, _MM_ROWS * (r + 1))
        acc_ref[rows, half_lanes[hf]] += jnp.dot(vT_ref[rows, :], wT_ref[:, half_lanes[hf]],
                                                 preferred_element_type=F32)

    def gate_group(hf, ii):
        rows = slice(N_KEYS * ii, N_KEYS * (ii + 1))
        for lt in range(lt_per_half):
            lanes = slice((hf * lt_per_half + lt) * LANE, (hf * lt_per_half + lt + 1) * LANE)
            gate = jnp.zeros((N_KEYS, LANE), F32)
            for h in range(PEER_HEADS):
                s1 = s1s_ref[h, pl.ds(i1_base, n_i1), lanes][ii:ii + 1]
                sm = s1 + s2T_ref[h, :, lanes]
                gate = gate + jnp.where(sm >= thr_ref[h, :, lanes], jnp.exp2(sm), 0.0)
            a = aT_ref[rows, lanes]
            act = 0.5 * a * (1.0 + lax.erf(a * _SQRT_HALF))
            wT_ref[rows, lanes] = (gate * act).astype(BF16)

    cuts = [0, N_KEYS, 2 * N_KEYS] + list(range(_MM_ROWS * (2 * N_KEYS // _MM_ROWS + 1), _EXPERT_BLOCK + 1, _MM_ROWS))
    pending = [functools.partial(act_piece, 0, r0, r1) for r0, r1 in zip(cuts[:-1], cuts[1:])]
    pending += [functools.partial(act_piece, hf, r0, r0 + _MM_ROWS) for hf in range(1, n_half)
                for r0 in range(0, _EXPERT_BLOCK, _MM_ROWS)]
    pending.pop(0)()
    pending.pop(0)()
    for hf in range(n_half):
        for ii in range(n_i1):
            if pending and (hf, ii) != (0, 0):
                pending.pop(0)()
            gate_group(hf, ii)
        pending.extend(functools.partial(out_piece, hf, r) for r in range(D_MODEL // _MM_ROWS))
    for piece in pending:
        piece()

    @pl.when(j == pl.num_programs(1) - 1)
    def _():
        x2 = x1_ref[...] + acc_ref[...].T
        y_ref[...] = x2 * lax.rsqrt(jnp.mean(x2 * x2, axis=-1, keepdims=True) + EPS) * gf_ref[...]


def _peer_dense(h2T, u_bf, vT_bf, s1s, s2T, thr, x1, lnf_g, tm):
    n = x1.shape[0]
    assert n % tm == 0
    ne = u_bf.shape[0] // _EXPERT_BLOCK
    gf = lnf_g.reshape(1, D_MODEL)
    sc = pl.BlockSpec((PEER_HEADS, N_KEYS, tm), lambda i, j: (0, 0, i))
    st = pl.BlockSpec((PEER_HEADS, 1, tm), lambda i, j: (0, 0, i))
    return pl.pallas_call(
        functools.partial(_peer_dense_kernel, tm=tm),
        grid=(n // tm, ne),
        in_specs=[pl.BlockSpec((D_MODEL, tm), lambda i, j: (0, i)),
                  pl.BlockSpec((_EXPERT_BLOCK, D_MODEL), lambda i, j: (j, 0)),
                  pl.BlockSpec((D_MODEL, _EXPERT_BLOCK), lambda i, j: (0, j)),
                  sc, sc, st,
                  pl.BlockSpec((tm, D_MODEL), lambda i, j: (i, 0)),
                  pl.BlockSpec((1, D_MODEL), lambda i, j: (0, 0))],
        out_specs=pl.BlockSpec((tm, D_MODEL), lambda i, j: (i, 0)),
        out_shape=jax.ShapeDtypeStruct((n, D_MODEL), F32),
        scratch_shapes=[pltpu.VMEM((_EXPERT_BLOCK, tm), F32),
                        pltpu.VMEM((_EXPERT_BLOCK, tm), BF16),
                        pltpu.VMEM((D_MODEL, tm), F32)],
        compiler_params=_params("parallel", "arbitrary"),
        name="peer_dense",
    )(h2T, u_bf, vT_bf, s1s, s2T, thr, x1, gf)


def _mix_to_output(x2d, oaT, ob, lw, tm_out, tl_sel, tm_dense):
    x1, h2T, s1T, s2T = _outproj(x2d, oaT, ob, lw["woa"], lw["wob"], lw["ln2_g"], lw["wq"],
                                 lw["k1"], lw["k2"], tm_out)
    thr, s1s = _peer_select(s1T, s2T, tl_sel)
    return _peer_dense(h2T, lw["u"], lw["vT"], s1s, s2T, thr, x1, lw["lnf_g"], tm_dense)


def _layer_prompt(x, lw):
    nb, t_len, _ = x.shape
    n = nb * t_len
    x2d = x.reshape(n, D_MODEL)
    ka, va, ki, qb, kb, vb, gb, qaT, qiT, vaT, wiT = _inproj(x2d, lw["ln1_g"], lw["wstd"], lw["wt"], 512)
    tq = 256
    nj = t_len // tq
    variants = tuple((tq * (j + 1), tq * j, tq * (j + 1)) for j in range(nj))
    oaT = _dsa(qaT, qiT, wiT, ka.reshape(nb, t_len, 128), ki.reshape(nb, t_len, 64), vaT,
               nb=nb, tq=tq, variants=variants, topk=min(TOPK_MAX, t_len // 4))
    s0 = jnp.zeros((nb, 2 * LANE, DV_B), F32)
    ob, s_fin = _retention(qb, kb, vb, gb, s0, jnp.arange(t_len, dtype=I32),
                           nb=nb, t_len=t_len, c_len=CHUNK, nsub=4)
    y = _mix_to_output(x2d, oaT, ob, lw, 512, 256, 512)
    return (y.reshape(nb, t_len, D_MODEL), ka.reshape(nb, t_len, KVH_A, DH_A),
            va.reshape(nb, t_len, KVH_A, DH_A), ki.reshape(nb, t_len, D_IDX), _state_from_packed(s_fin))


def _layer_sample(x, ck, cv, cki, cs, lw):
    nb, t_len, _ = x.shape
    past = ck.shape[1]
    n = nb * t_len
    x2d = x.reshape(n, D_MODEL)
    ka, va, ki, qb, kb, vb, gb, qaT, qiT, vaT, wiT = _inproj(x2d, lw["ln1_g"], lw["wstd"], lw["wt"], n)
    total = past + t_len
    new_block = lambda a: jnp.pad(a.reshape(nb, t_len, -1), ((0, 0), (0, LANE - t_len), (0, 0)))
    per_head = lambda a, d: a.reshape(H_A, d, nb, t_len).transpose(2, 0, 3, 1)
    qa = per_head(qaT, DH_A) * (DH_A ** -0.5)
    group_cols = (jnp.arange(KVH_A * DH_A) // DH_A)[None, :] == (jnp.arange(H_A) // GRP_A)[:, None]
    qa_rows = jnp.where(group_cols[None, :, None, :], jnp.concatenate([qa] * KVH_A, axis=-1), 0.0)
    oa = _dsa_few(qa_rows.reshape(nb, H_A * t_len, KVH_A * DH_A),
                  per_head(qiT, D_IDX).reshape(nb, H_IDX * t_len, D_IDX),
                  wiT.reshape(H_IDX, nb, t_len).transpose(1, 2, 0),
                  ck.reshape(nb, past, KVH_A * DH_A), cki, cv.reshape(nb, past, KVH_A * DH_A),
                  new_block(ka), new_block(ki), new_block(va),
                  nq=t_len, qpos0=past, n_valid=total, topk=min(TOPK_MAX, total // 4))
    oaT = oa.T
    ob, s_new = _retention(qb, kb, vb, gb, _state_to_packed(cs), past + jnp.arange(t_len, dtype=I32),
                           nb=nb, t_len=t_len, c_len=t_len, nsub=1)
    y = _mix_to_output(x2d, oaT, ob, lw, n, n, n)
    return (y.reshape(nb, t_len, D_MODEL), ka.reshape(nb, t_len, KVH_A, DH_A),
            va.reshape(nb, t_len, KVH_A, DH_A), ki.reshape(nb, t_len, D_IDX), _state_from_packed(s_new))


def kernel(x_prompt, x_sample, cache_attn_k, cache_attn_v, cache_idx_k, state_retention,
           ln1_g, w_in, w_out, ln2_g, peer_w_query, peer_keys1, peer_keys2, peer_u, peer_v, ln_final_g):
    depth = w_in.shape[0]
    assert depth == 1, "the final norm is fused into the (single) layer's last kernel"
    wstd, wt = _prep_inproj_weights(w_in[0])
    lw = dict(ln1_g=ln1_g[0], wstd=wstd, wt=wt,
              woa=w_out[0, :W_A].astype(BF16), wob=w_out[0, W_A:].astype(BF16),
              ln2_g=ln2_g[0], wq=peer_w_query[0].astype(BF16),
              k1=peer_keys1[0], k2=peer_keys2[0],
              u=peer_u[0].astype(BF16), vT=peer_v[0].T.astype(BF16), lnf_g=ln_final_g)
    yp, kp, vp, kip, sp = _layer_prompt(x_prompt, lw)
    ys, ks, vs, kis, ss = _layer_sample(x_sample, cache_attn_k[0], cache_attn_v[0], cache_idx_k[0],
                                        state_retention[0], lw)
    st = lambda a: a[None]
    return (yp, ys, st(kp), st(vp), st(kip), st(sp), st(ks), st(vs), st(kis), st(ss))
```
